```python
import math
import jax, jax.numpy as jnp
from jax import lax
import numpy as np

D_MODEL = 1024
BATCH = 4
SEQ = 8192
DEPTH = 4

N_META = 16
BLOCK = 128
WINDOW = 128
ATT_HEADS = 8
ATT_KV_HEADS = 2
HEAD_DIM = 64
SSM_HEADS = 16
SSM_HEAD_DIM = 64
SSM_INNER = SSM_HEADS * SSM_HEAD_DIM
SSM_GROUPS = 2
SSM_STATE = 64
SSM_CONV = 4
CONF_DIM = D_MODEL
CONF_KERNEL = 31
D_FF = 4 * D_MODEL
EPS = 1e-6
LN_EPS = 1e-5

Q_W = ATT_HEADS * HEAD_DIM
KV_W = ATT_KV_HEADS * HEAD_DIM
BC_W = SSM_GROUPS * SSM_STATE
XBC_W = SSM_INNER + 2 * BC_W
S_Q = Q_W
S_K = S_Q + KV_W
S_V = S_K + KV_W
S_Z = S_V + SSM_INNER
S_XBC = S_Z + XBC_W
IN_W = S_XBC + SSM_HEADS
MIX_W = Q_W + SSM_INNER
N_EVEN = (DEPTH + 1) // 2
N_ODD = DEPTH // 2
FRONT_PAD = BLOCK - N_META

kernel_name = "hybrid_swa_ssd_conformer_trunk"


def rms_norm(x, g):
    xf = x.astype(jnp.float32)
    y = xf * lax.rsqrt(jnp.mean(xf * xf, -1, keepdims=True) + EPS)
    return (y * g.astype(jnp.float32)).astype(x.dtype)


def layer_norm(x, g, b):
    xf = x.astype(jnp.float32)
    mu = jnp.mean(xf, -1, keepdims=True)
    var = jnp.mean(jnp.square(xf - mu), -1, keepdims=True)
    y = (xf - mu) * lax.rsqrt(var + LN_EPS)
    return (y * g.astype(jnp.float32) + b.astype(jnp.float32)).astype(x.dtype)


def causal_depthwise_conv(x, w, b):
    k = w.shape[0]
    y = lax.conv_general_dilated(x, w[:, None, :].astype(x.dtype), window_strides=(1,),
                                 padding=((k - 1, 0),), dimension_numbers=('NWC', 'WIO', 'NWC'),
                                 feature_group_count=x.shape[-1])
    return y + b.astype(x.dtype)


def front_pad(a, pad):
    return jnp.pad(a, [(0, 0), (pad, 0)] + [(0, 0)] * (a.ndim - 2))


def alibi_slopes(n):
    return jnp.exp2(-8.0 * jnp.arange(1, n + 1, dtype=jnp.float32) / n)


def sliding_window_attention(q, k, v, sinks):
    bsz = q.shape[0]
    grp = ATT_HEADS // ATT_KV_HEADS
    qb = front_pad(q, FRONT_PAD).reshape(bsz, -1, BLOCK, ATT_KV_HEADS, grp, HEAD_DIM)
    kb = front_pad(k, FRONT_PAD).reshape(bsz, -1, BLOCK, ATT_KV_HEADS, HEAD_DIM)
    vb = front_pad(v, FRONT_PAD).reshape(bsz, -1, BLOCK, ATT_KV_HEADS, HEAD_DIM)
    nb = kb.shape[1]
    shift = ((0, 0), (1, 0), (0, 0), (0, 0), (0, 0))
    kprev = jnp.pad(kb, shift)[:, :-1]
    vprev = jnp.pad(vb, shift)[:, :-1]
    kmeta = jnp.broadcast_to(k[:, None, :N_META], (bsz, nb, N_META, ATT_KV_HEADS, HEAD_DIM))
    vmeta = jnp.broadcast_to(v[:, None, :N_META], (bsz, nb, N_META, ATT_KV_HEADS, HEAD_DIM))
    keys = jnp.concatenate([kmeta, kprev, kb], axis=2)
    vals = jnp.concatenate([vmeta, vprev, vb], axis=2)
    nk = N_META + 2 * BLOCK
    q_pos = jnp.arange(nb)[:, None] * BLOCK + jnp.arange(BLOCK)[None, :] - FRONT_PAD
    meta_pos = jnp.broadcast_to(jnp.arange(N_META)[None, :], (nb, N_META))
    k_pos = jnp.concatenate([meta_pos, q_pos - BLOCK, q_pos], axis=1)
    is_meta = (jnp.arange(nk) < N_META)[None, None, :]
    dist = q_pos[:, :, None] - k_pos[:, None, :]
    valid = jnp.where(is_meta, dist >= 0,
                      (k_pos[:, None, :] >= N_META) & (dist >= 0) & (dist < WINDOW))
    pen_dist = jnp.where(is_meta, jnp.minimum(jnp.abs(dist), WINDOW), jnp.abs(dist)).astype(jnp.float32)
    slopes = alibi_slopes(ATT_HEADS).reshape(ATT_KV_HEADS, grp)
    bias = -slopes[None, :, :, None, None] * pen_dist[:, None, None]
    s = jnp.einsum('bnqkgd,bnskd->bnkgqs', qb, keys).astype(jnp.float32) * (HEAD_DIM ** -0.5)
    s = jnp.where(valid[:, None, None], s + bias, -1e30)
    sink = jnp.broadcast_to(sinks.astype(jnp.float32).reshape(ATT_KV_HEADS, grp)[None, None, :, :, None, None],
                            s.shape[:-1] + (1,))
    p = jax.nn.softmax(jnp.concatenate([s, sink], axis=-1), axis=-1)[..., :-1]
    o = jnp.einsum('bnkgqs,bnskd->bnqkgd', p.astype(vals.dtype), vals)
    return o.reshape(bsz, nb * BLOCK, Q_W)[:, FRONT_PAD:]


def segsum(a):
    cs = jnp.cumsum(a, axis=-1)
    d = cs[..., :, None] - cs[..., None, :]
    t = a.shape[-1]
    return jnp.where(jnp.tril(jnp.ones((t, t), bool)), d, -jnp.inf)


def ssd_scan(x, dt, a, b_mat, c_mat):
    out_dtype = x.dtype
    f32 = jnp.float32
    bsz, lp = x.shape[:2]
    nc = lp // BLOCK
    hpg = SSM_HEADS // SSM_GROUPS
    xc = x.astype(f32).reshape(bsz, nc, BLOCK, SSM_GROUPS, hpg, SSM_HEAD_DIM)
    dtc = dt.astype(f32).reshape(bsz, nc, BLOCK, SSM_GROUPS, hpg)
    bc = b_mat.astype(f32).reshape(bsz, nc, BLOCK, SSM_GROUPS, SSM_STATE)
    cc = c_mat.astype(f32).reshape(bsz, nc, BLOCK, SSM_GROUPS, SSM_STATE)
    dt_t = jnp.moveaxis(dtc, 2, -1)
    adt = dt_t * a.astype(f32).reshape(SSM_GROUPS, hpg)[None, None, :, :, None]
    a_cs = jnp.cumsum(adt, axis=-1)
    cb = jnp.einsum('bclgn,bcsgn->bcgls', cc, bc)
    w = cb[:, :, :, None] * jnp.exp(segsum(adt)) * dt_t[:, :, :, :, None, :]
    y_diag = jnp.einsum('bcghls,bcsghp->bclghp', w, xc)
    wx = jnp.exp(a_cs[..., -1:] - a_cs) * dt_t
    states = jnp.einsum('bclgn,bcghl,bclghp->bcghpn', bc, wx, xc)
    chunk_decay = jnp.exp(a_cs[..., -1])

    def step(h, inp):
        st, dec = inp
        return h * dec[..., None, None] + st, h

    h0 = jnp.zeros((bsz, SSM_GROUPS, hpg, SSM_HEAD_DIM, SSM_STATE), f32)
    _, prev = lax.scan(step, h0, (jnp.moveaxis(states, 1, 0), jnp.moveaxis(chunk_decay, 1, 0)))
    prev = jnp.moveaxis(prev, 0, 1)
    y_off = jnp.einsum('bclgn,bcghpn,bcghl->bclghp', cc, prev, jnp.exp(a_cs))
    y = (y_diag + y_off).reshape(bsz, lp, SSM_HEADS, SSM_HEAD_DIM)
    return y.astype(out_dtype)


def mamba2_branch(z, xbc, dt_raw, conv_w, conv_b, dt_bias, a_log, d_skip, norm_w):
    bsz, seqlen = z.shape[:2]
    xbc = jax.nn.silu(causal_depthwise_conv(xbc, conv_w, conv_b))
    xs, bm, cm = jnp.split(xbc, [SSM_INNER, SSM_INNER + BC_W], axis=-1)
    xs = xs.reshape(bsz, seqlen, SSM_HEADS, SSM_HEAD_DIM)
    bm = bm.reshape(bsz, seqlen, SSM_GROUPS, SSM_STATE)
    cm = cm.reshape(bsz, seqlen, SSM_GROUPS, SSM_STATE)
    dt = jax.nn.softplus(dt_raw + dt_bias.astype(dt_raw.dtype))
    a = -jnp.exp(a_log.astype(jnp.float32))
    y = ssd_scan(front_pad(xs, FRONT_PAD), front_pad(dt, FRONT_PAD), a,
                 front_pad(bm, FRONT_PAD), front_pad(cm, FRONT_PAD))[:, FRONT_PAD:]
    y = y + xs * d_skip.astype(xs.dtype)[:, None]
    y = y.reshape(bsz, seqlen, SSM_INNER) * jax.nn.silu(z)
    yg = y.astype(jnp.float32).reshape(bsz, seqlen, SSM_GROUPS, SSM_INNER // SSM_GROUPS)
    yg = yg * lax.rsqrt(jnp.mean(yg * yg, -1, keepdims=True) + EPS)
    return (yg.reshape(bsz, seqlen, SSM_INNER) * norm_w.astype(jnp.float32)).astype(z.dtype)


def even_mixer(h, norm_g, w_in, conv_w, conv_b, dt_bias, a_log, d_skip, ssm_norm_w, q_norm, k_norm, sinks, w_out):
    bsz, seqlen = h.shape[:2]
    u = rms_norm(h, norm_g)
    proj = u @ w_in
    q, k, v, z, xbc, dt_raw = jnp.split(proj, [S_Q, S_K, S_V, S_Z, S_XBC], axis=-1)
    q = rms_norm(q.reshape(bsz, seqlen, ATT_HEADS, HEAD_DIM), q_norm)
    k = rms_norm(k.reshape(bsz, seqlen, ATT_KV_HEADS, HEAD_DIM), k_norm)
    v = v.reshape(bsz, seqlen, ATT_KV_HEADS, HEAD_DIM)
    att = sliding_window_attention(q, k, v, sinks)
    ssm = mamba2_branch(z, xbc, dt_raw, conv_w, conv_b, dt_bias, a_log, d_skip, ssm_norm_w)
    return jnp.concatenate([att, ssm], axis=-1) @ w_out


def conformer_conv_module(h, norm_g, pw1_w, pw1_b, dw_w, dw_b, ln_g, ln_b, pw2_w, pw2_b):
    u = rms_norm(h, norm_g) @ pw1_w + pw1_b
    u = u[..., :CONF_DIM] * jax.nn.sigmoid(u[..., CONF_DIM:])
    u = causal_depthwise_conv(u, dw_w, dw_b)
    u = jax.nn.silu(layer_norm(u, ln_g, ln_b))
    return u @ pw2_w + pw2_b


def sq_relu_mlp(h, norm_g, w_up, w_down):
    return jnp.square(jax.nn.relu(rms_norm(h, norm_g) @ w_up)) @ w_down


def setup_inputs(seed: int = 0) -> dict:
    key = jax.random.key(seed)
    ks = iter(jax.random.split(key, 40))

    def nrm(shape, scale):
        return scale * jax.random.normal(next(ks), shape, jnp.float32)

    def gain(shape):
        return 1.0 + nrm(shape, 0.02)

    x = nrm((BATCH, SEQ, D_MODEL), 1.0)
    meta_tokens = nrm((N_META, D_MODEL), 1.0)
    mix_norm_even = gain((N_EVEN, D_MODEL))
    w_in = nrm((N_EVEN, D_MODEL, IN_W), D_MODEL ** -0.5)
    ssm_conv_w = nrm((N_EVEN, SSM_CONV, XBC_W), SSM_CONV ** -0.5)
    ssm_conv_b = nrm((N_EVEN, XBC_W), 0.02)
    dt0 = jnp.exp(jax.random.uniform(next(ks), (N_EVEN, SSM_HEADS), jnp.float32,
                                     minval=math.log(1e-3), maxval=math.log(1e-1)))
    dt_bias = dt0 + jnp.log(-jnp.expm1(-dt0))
    a_log = jnp.log(jax.random.uniform(next(ks), (N_EVEN, SSM_HEADS), jnp.float32, minval=1.0, maxval=16.0))
    d_skip = gain((N_EVEN, SSM_HEADS))
    ssm_norm_w = gain((N_EVEN, SSM_INNER))
    q_norm = gain((N_EVEN, HEAD_DIM))
    k_norm = gain((N_EVEN, HEAD_DIM))
    sinks = nrm((N_EVEN, ATT_HEADS), 0.5)
    w_out = nrm((N_EVEN, MIX_W, D_MODEL), MIX_W ** -0.5)
    mix_norm_odd = gain((N_ODD, D_MODEL))
    pw1_w = nrm((N_ODD, D_MODEL, 2 * CONF_DIM), D_MODEL ** -0.5)
    pw1_b = nrm((N_ODD, 2 * CONF_DIM), 0.02)
    dw_w = nrm((N_ODD, CONF_KERNEL, CONF_DIM), CONF_KERNEL ** -0.5)
    dw_b = nrm((N_ODD, CONF_DIM), 0.02)
    ln_g = gain((N_ODD, CONF_DIM))
    ln_b = nrm((N_ODD, CONF_DIM), 0.02)
    pw2_w = nrm((N_ODD, CONF_DIM, D_MODEL), CONF_DIM ** -0.5)
    pw2_b = nrm((N_ODD, D_MODEL), 0.02)
    mlp_norm = gain((DEPTH, D_MODEL))
    w_up = nrm((DEPTH, D_MODEL, D_FF), D_MODEL ** -0.5)
    w_down = nrm((DEPTH, D_FF, D_MODEL), D_FF ** -0.5)
    return {"x": x, "meta_tokens": meta_tokens, "mix_norm_even": mix_norm_even, "w_in": w_in,
            "ssm_conv_w": ssm_conv_w, "ssm_conv_b": ssm_conv_b, "dt_bias": dt_bias, "a_log": a_log,
            "d_skip": d_skip, "ssm_norm_w": ssm_norm_w, "q_norm": q_norm, "k_norm": k_norm,
            "sinks": sinks, "w_out": w_out, "mix_norm_odd": mix_norm_odd, "pw1_w": pw1_w,
            "pw1_b": pw1_b, "dw_w": dw_w, "dw_b": dw_b, "ln_g": ln_g, "ln_b": ln_b,
            "pw2_w": pw2_w, "pw2_b": pw2_b, "mlp_norm": mlp_norm, "w_up": w_up, "w_down": w_down}


def reference(x, meta_tokens, mix_norm_even, w_in, ssm_conv_w, ssm_conv_b, dt_bias, a_log, d_skip,
              ssm_norm_w, q_norm, k_norm, sinks, w_out, mix_norm_odd, pw1_w, pw1_b, dw_w, dw_b,
              ln_g, ln_b, pw2_w, pw2_b, mlp_norm, w_up, w_down):
    bsz = x.shape[0]
    meta = jnp.broadcast_to(meta_tokens[None].astype(x.dtype), (bsz, N_META, D_MODEL))
    h = jnp.concatenate([meta, x], axis=1)
    for layer in range(DEPTH):
        i = layer // 2
        if layer % 2 == 0:
            h = h + even_mixer(h, mix_norm_even[i], w_in[i], ssm_conv_w[i], ssm_conv_b[i], dt_bias[i],
                               a_log[i], d_skip[i], ssm_norm_w[i], q_norm[i], k_norm[i], sinks[i], w_out[i])
        else:
            h = h + conformer_conv_module(h, mix_norm_odd[i], pw1_w[i], pw1_b[i], dw_w[i], dw_b[i],
                                          ln_g[i], ln_b[i], pw2_w[i], pw2_b[i])
        h = h + sq_relu_mlp(h, mlp_norm[layer], w_up[layer], w_down[layer])
    return h[:, N_META:]
```

```python
import functools

import jax
import jax.numpy as jnp
from jax import lax
from jax.experimental import pallas as pl
from jax.experimental.pallas import tpu as pltpu

D_MODEL = 1024
N_META = 16
BLOCK = 128
WINDOW = 128
ATT_HEADS = 8
ATT_KV_HEADS = 2
ATT_GROUP = ATT_HEADS // ATT_KV_HEADS
HEAD_DIM = 64
SSM_HEADS = 16
SSM_HEAD_DIM = 64
SSM_INNER = SSM_HEADS * SSM_HEAD_DIM
SSM_GROUPS = 2
SSM_STATE = 64
SSM_CONV = 4
CONF_KERNEL = 31
D_FF = 4 * D_MODEL
EPS = 1e-6
LN_EPS = 1e-5

Q_W = ATT_HEADS * HEAD_DIM
KV_W = ATT_KV_HEADS * HEAD_DIM
BC_W = SSM_GROUPS * SSM_STATE
XBC_W = SSM_INNER + 2 * BC_W
FRONT_PAD = BLOCK - N_META

LANES = 128
HALO = 32
CONV_ROWS = 128
FF_CHUNK = 1024
VMEM_LIMIT = 56 * 1024 * 1024
NEG = -1e30

BF16 = jnp.bfloat16
F32 = jnp.float32


def _const_spec(shape):
    zeros = (0,) * len(shape)
    return pl.BlockSpec(shape, lambda *_: zeros, pipeline_mode=pl.Buffered(1))


def _params(n_axes, parallel=True):
    sem = ("parallel" if parallel else "arbitrary",) * n_axes
    return pltpu.CompilerParams(dimension_semantics=sem, vmem_limit_bytes=VMEM_LIMIT)


def _rms(x, g):
    return x * lax.rsqrt(jnp.mean(x * x, axis=-1, keepdims=True) + EPS) * g


def _sigmoid(x):
    return 1.0 / (1.0 + jnp.exp(-x))


def _silu(x):
    return x * _sigmoid(x)


def _head_rms(x, gain):
    lane = lax.broadcasted_iota(jnp.int32, (1, LANES), 1)
    low = lane < HEAD_DIM
    outs = []
    for t in range(x.shape[1] // LANES):
        xt = x[:, t * LANES:(t + 1) * LANES]
        sq = xt * xt
        lo = jnp.sum(jnp.where(low, sq, 0.0), axis=-1, keepdims=True)
        hi = jnp.sum(jnp.where(low, 0.0, sq), axis=-1, keepdims=True)
        r = jnp.where(low, lax.rsqrt(lo * (1.0 / HEAD_DIM) + EPS), lax.rsqrt(hi * (1.0 / HEAD_DIM) + EPS))
        outs.append(xt * r * gain[:, t * LANES:(t + 1) * LANES])
    return outs[0] if len(outs) == 1 else jnp.concatenate(outs, axis=1)


def _split3(x):
    h1 = x.astype(BF16)
    r1 = x - h1.astype(F32)
    h2 = r1.astype(BF16)
    h3 = (r1 - h2.astype(F32)).astype(BF16)
    return h1, h2, h3


IN_COLS = Q_W + 2 * LANES + SSM_INNER + XBC_W + LANES


def _inproj_kernel(h_ref, g_ref, w_ref, qg_ref, kg_ref, q_ref, kv_ref, z_ref, xbc_ref, dt_ref):
    u = _rms(h_ref[...], g_ref[...]).astype(BF16)
    p = jnp.dot(u, w_ref[...], preferred_element_type=F32)
    o = 0
    q = _head_rms(p[:, o:o + Q_W], qg_ref[...])
    o += Q_W
    k = _head_rms(p[:, o:o + LANES], kg_ref[...])
    o += LANES
    v = p[:, o:o + LANES]
    o += LANES
    q_ref[...] = q.astype(BF16)
    kv_ref[...] = jnp.concatenate([k, v], axis=1).astype(BF16)
    z_ref[...] = p[:, o:o + SSM_INNER]
    o += SSM_INNER
    xbc_ref[...] = p[:, o:o + XBC_W]
    o += XBC_W
    dt_ref[...] = p[:, o:o + LANES]


def _inproj(h, g, w, qg, kg, tm):
    rows = h.shape[0]
    row = lambda width: pl.BlockSpec((tm, width), lambda i: (i, 0))
    return pl.pallas_call(
        _inproj_kernel,
        grid=(rows // tm,),
        in_specs=[row(D_MODEL), _const_spec((1, D_MODEL)), _const_spec((D_MODEL, IN_COLS)),
                  _const_spec((1, Q_W)), _const_spec((1, LANES))],
        out_specs=[row(Q_W), row(2 * LANES), row(SSM_INNER), row(XBC_W), row(LANES)],
        out_shape=[jax.ShapeDtypeStruct((rows, Q_W), BF16), jax.ShapeDtypeStruct((rows, 2 * LANES), BF16),
                   jax.ShapeDtypeStruct((rows, SSM_INNER), F32), jax.ShapeDtypeStruct((rows, XBC_W), F32),
                   jax.ShapeDtypeStruct((rows, LANES), F32)],
        compiler_params=_params(1),
        name="inproj",
    )(h, g, w, qg, kg)


def _attn_kernel(sink_ref, q_ref, kvc_ref, kvp_ref, kvm_ref, o_ref):
    n = pl.program_id(1)
    q = q_ref[0]
    qs = jnp.concatenate([q[:, g * LANES:(g + 1) * LANES] for g in range(ATT_GROUP)], axis=0)
    kvc = kvc_ref[0]
    kvp = kvp_ref[0]
    kvm = kvm_ref[0, FRONT_PAD:BLOCK, :]
    k_main = jnp.concatenate([kvp[:, :LANES], kvc[:, :LANES]], axis=0)
    v_main = jnp.concatenate([kvp[:, LANES:], kvc[:, LANES:]], axis=0)
    k_meta = kvm[:, :LANES]
    v_meta = kvm[:, LANES:]

    rows = ATT_GROUP * BLOCK
    row = lax.broadcasted_iota(jnp.int32, (rows, 1), 0)
    grp = row // BLOCK
    q_pos = n * BLOCK + (row % BLOCK) - FRONT_PAD
    key = lax.broadcasted_iota(jnp.int32, (1, 2 * BLOCK), 1)
    k_pos = (n - 1) * BLOCK + key - FRONT_PAD
    dist = q_pos - k_pos
    valid = (k_pos >= N_META) & (dist >= 0) & (dist < WINDOW)
    pen = jnp.abs(dist).astype(F32)
    meta = lax.broadcasted_iota(jnp.int32, (1, N_META), 1)
    dist_m = q_pos - meta
    valid_m = dist_m >= 0
    pen_m = jnp.minimum(jnp.abs(dist_m), WINDOW).astype(F32)

    lane = lax.broadcasted_iota(jnp.int32, (1, LANES), 1)
    nt = (((1,), (1,)), ((), ()))
    out = None
    for kv in range(ATT_KV_HEADS):
        half = (lane < HEAD_DIM) if kv == 0 else (lane >= HEAD_DIM)
        slope = jnp.zeros((rows, 1), F32)
        sink = jnp.zeros((rows, 1), F32)
        for g in range(ATT_GROUP):
            head = kv * ATT_GROUP + g
            slope = jnp.where(grp == g, 2.0 ** (-8.0 * (head + 1) / ATT_HEADS), slope)
            sink = jnp.where(grp == g, sink_ref[head], sink)
        km = jnp.where(half, k_main, jnp.zeros_like(k_main))
        kmm = jnp.where(half, k_meta, jnp.zeros_like(k_meta))
        s = lax.dot_general(qs, km, nt, preferred_element_type=F32)
        s = jnp.where(valid, s - slope * pen, NEG)
        sm = lax.dot_general(qs, kmm, nt, preferred_element_type=F32)
        sm = jnp.where(valid_m, sm - slope * pen_m, NEG)
        mx = jnp.maximum(jnp.maximum(jnp.max(s, axis=-1, keepdims=True), jnp.max(sm, axis=-1, keepdims=True)), sink)
        e = jnp.exp(s - mx)
        em = jnp.exp(sm - mx)
        den = jnp.sum(e, axis=-1, keepdims=True) + jnp.sum(em, axis=-1, keepdims=True) + jnp.exp(sink - mx)
        o = (jnp.dot(e.astype(BF16), v_main, preferred_element_type=F32)
             + jnp.dot(em.astype(BF16), v_meta, preferred_element_type=F32))
        o = o * (1.0 / den)
        out = o if out is None else jnp.where(lane < HEAD_DIM, out, o)
    o_ref[0] = jnp.concatenate([out[g * BLOCK:(g + 1) * BLOCK] for g in range(ATT_GROUP)], axis=1).astype(BF16)


def _attention(sinks, q, kv):
    bsz, lp, _ = q.shape
    nb = lp // BLOCK
    return pl.pallas_call(
        _attn_kernel,
        grid=(bsz, nb),
        in_specs=[pl.BlockSpec(memory_space=pltpu.SMEM),
                  pl.BlockSpec((1, BLOCK, Q_W), lambda b, n: (b, n, 0)),
                  pl.BlockSpec((1, BLOCK, 2 * LANES), lambda b, n: (b, n, 0)),
                  pl.BlockSpec((1, BLOCK, 2 * LANES), lambda b, n: (b, jnp.maximum(n - 1, 0), 0)),
                  pl.BlockSpec((1, BLOCK, 2 * LANES), lambda b, n: (b, 0, 0))],
        out_specs=pl.BlockSpec((1, BLOCK, Q_W), lambda b, n: (b, n, 0)),
        out_shape=jax.ShapeDtypeStruct((bsz, lp, Q_W), BF16),
        compiler_params=_params(2),
        name="attention",
    )(sinks, q, kv, kv, kv)


def _ssd_kernel(xbc_ref, dt_ref, z_ref, cw_ref, cb_ref, dtb_ref, alog_ref, dsk_ref, nw_ref, e_ref,
                o_ref, state, xpad):
    c = pl.program_id(1)

    @pl.when(c == 0)
    def _():
        state[...] = jnp.zeros_like(state)
        xpad[0:8, :] = jnp.zeros((8, XBC_W), F32)

    row = lax.broadcasted_iota(jnp.int32, (BLOCK, 1), 0)
    valid = (c > 0) | (row >= FRONT_PAD)

    x_raw = jnp.where(valid, xbc_ref[0], 0.0)
    xpad[8:8 + BLOCK, :] = x_raw
    acc = cb_ref[...] + cw_ref[SSM_CONV - 1:SSM_CONV, :] * x_raw
    for s in range(1, SSM_CONV):
        acc = acc + cw_ref[SSM_CONV - 1 - s:SSM_CONV - s, :] * xpad[8 - s:8 - s + BLOCK, :]
    xpad[0:8, :] = x_raw[BLOCK - 8:BLOCK, :]
    xa = jnp.where(valid, _silu(acc), 0.0)
    xs = xa[:, :SSM_INNER]
    bm = xa[:, SSM_INNER:SSM_INNER + BC_W]
    cm = xa[:, SSM_INNER + BC_W:]

    dtr = dt_ref[0] + dtb_ref[...]
    dt = jnp.where(valid, jnp.maximum(dtr, 0.0) + jnp.log1p(jnp.exp(-jnp.abs(dtr))), 0.0)
    a = -jnp.exp(alog_ref[...])
    adt = dt * a
    ri = lax.broadcasted_iota(jnp.int32, (BLOCK, BLOCK), 0)
    ci = lax.broadcasted_iota(jnp.int32, (BLOCK, BLOCK), 1)
    tril = ri >= ci
    tril_b = jnp.where(tril, 1.0, 0.0).astype(BF16)
    a_cs = sum(jnp.dot(tril_b, t, preferred_element_type=F32) for t in _split3(adt))
    a_last = a_cs[BLOCK - 1:BLOCK, :]
    wx = jnp.exp(a_last - a_cs) * dt
    ea = jnp.exp(a_cs)
    e_mat = e_ref[...]
    both = jnp.concatenate([wx, ea], axis=0)
    both_e = sum(jnp.dot(t, e_mat, preferred_element_type=F32) for t in _split3(both))
    wx_e = both_e[:BLOCK]
    ea_e = both_e[BLOCK:]
    decay_e = ea_e[BLOCK - 1:BLOCK, :]
    a_cs_t = a_cs.T
    dt_t = dt.T

    lane = lax.broadcasted_iota(jnp.int32, (1, LANES), 1)
    low = lane < SSM_STATE
    cm_b = cm.astype(BF16)
    bm_b = bm.astype(BF16)
    zero_b = jnp.zeros_like(cm_b)
    nt = (((1,), (1,)), ((), ()))
    cb = [lax.dot_general(jnp.where(low, cm_b, zero_b), bm_b, nt, preferred_element_type=F32),
          lax.dot_general(jnp.where(low, zero_b, cm_b), bm_b, nt, preferred_element_type=F32)]

    hpg = SSM_HEADS // SSM_GROUPS
    y_tiles = []
    for j in range(SSM_HEADS // 2):
        xp = xs[:, j * LANES:(j + 1) * LANES].astype(BF16)
        zx = jnp.zeros_like(xp)
        rhs = jnp.concatenate([jnp.where(low, xp, zx), jnp.where(low, zx, xp)], axis=0)
        ws = []
        for hd in (2 * j, 2 * j + 1):
            seg = a_cs[:, hd:hd + 1] - a_cs_t[hd:hd + 1, :]
            decay = jnp.exp(jnp.where(tril, seg, NEG))
            ws.append((cb[hd // hpg] * decay * dt_t[hd:hd + 1, :]).astype(BF16))
        y_tiles.append(jnp.dot(jnp.concatenate(ws, axis=1), rhs, preferred_element_type=F32))
    y = jnp.concatenate(y_tiles, axis=1)

    st = state[...]
    y = y + jnp.dot(cm_b, st.astype(BF16), preferred_element_type=F32) * ea_e
    upd = jnp.dot(bm.T.astype(BF16), (xs * wx_e).astype(BF16), preferred_element_type=F32)
    sr = lax.broadcasted_iota(jnp.int32, (BLOCK, SSM_INNER), 0)
    sc = lax.broadcasted_iota(jnp.int32, (BLOCK, SSM_INNER), 1)
    same_group = (sr < SSM_STATE) == (sc < SSM_INNER // SSM_GROUPS)
    state[...] = st * decay_e + jnp.where(same_group, upd, 0.0)

    y = y + xs * dsk_ref[...]
    y = y * _silu(z_ref[0])
    gw = SSM_INNER // SSM_GROUPS
    outs = []
    for g in range(SSM_GROUPS):
        yg = y[:, g * gw:(g + 1) * gw]
        outs.append(yg * lax.rsqrt(jnp.mean(yg * yg, axis=-1, keepdims=True) + EPS))
    o_ref[0] = (jnp.concatenate(outs, axis=1) * nw_ref[...]).astype(BF16)


def _ssd(xbc, dt, z, cw, cb, dtb, alog, dsk, nw, e_mat):
    bsz, lp, _ = xbc.shape
    nb = lp // BLOCK
    blk = lambda width: pl.BlockSpec((1, BLOCK, width), lambda b, c: (b, c, 0))
    return pl.pallas_call(
        _ssd_kernel,
        grid=(bsz, nb),
        in_specs=[blk(XBC_W), blk(LANES), blk(SSM_INNER),
                  _const_spec((SSM_CONV, XBC_W)), _const_spec((1, XBC_W)), _const_spec((1, LANES)),
                  _const_spec((1, LANES)), _const_spec((1, SSM_INNER)), _const_spec((1, SSM_INNER)),
                  _const_spec((LANES, SSM_INNER))],
        out_specs=blk(SSM_INNER),
        out_shape=jax.ShapeDtypeStruct((bsz, lp, SSM_INNER), BF16),
        scratch_shapes=[pltpu.VMEM((BLOCK, SSM_INNER), F32), pltpu.VMEM((8 + BLOCK, XBC_W), F32)],
        compiler_params=pltpu.CompilerParams(dimension_semantics=("parallel", "arbitrary"),
                                             vmem_limit_bytes=VMEM_LIMIT),
        name="ssd",
    )(xbc, dt, z, cw, cb, dtb, alog, dsk, nw, e_mat)


def _mlp_tail(o_ref, g_ref, wu_ref, wd_ref):
    x = o_ref[...]
    u = _rms(x, g_ref[...]).astype(BF16)
    acc = None
    for c in range(D_FF // FF_CHUNK):
        a = jnp.dot(u, wu_ref[:, c * FF_CHUNK:(c + 1) * FF_CHUNK], preferred_element_type=F32)
        a = jnp.maximum(a, 0.0)
        a = (a * a).astype(BF16)
        d = jnp.dot(a, wd_ref[c * FF_CHUNK:(c + 1) * FF_CHUNK, :], preferred_element_type=F32)
        acc = d if acc is None else acc + d
    o_ref[...] = x + acc


def _post_even_kernel(h_ref, att_ref, ssm_ref, wa_ref, ws_ref, g_ref, wu_ref, wd_ref, o_ref):
    o_ref[...] = (h_ref[...] + jnp.dot(att_ref[...], wa_ref[...], preferred_element_type=F32)
                  + jnp.dot(ssm_ref[...], ws_ref[...], preferred_element_type=F32))
    _mlp_tail(o_ref, g_ref, wu_ref, wd_ref)


def _post_even(h, att, ssm, wa, ws, g, wu, wd, tm):
    rows = h.shape[0]
    row = lambda width: pl.BlockSpec((tm, width), lambda i: (i, 0))
    return pl.pallas_call(
        _post_even_kernel,
        grid=(rows // tm,),
        in_specs=[row(D_MODEL), row(Q_W), row(SSM_INNER), _const_spec((Q_W, D_MODEL)),
                  _const_spec((SSM_INNER, D_MODEL)), _const_spec((1, D_MODEL)),
                  _const_spec((D_MODEL, D_FF)), _const_spec((D_FF, D_MODEL))],
        out_specs=row(D_MODEL),
        out_shape=jax.ShapeDtypeStruct((rows, D_MODEL), F32),
        compiler_params=_params(1),
        name="post_even",
    )(h, att, ssm, wa, ws, g, wu, wd)


def _glu_kernel(h_ref, g_ref, w_ref, b_ref, o_ref):
    u = _rms(h_ref[...], g_ref[...]).astype(BF16)
    p = jnp.dot(u, w_ref[...], preferred_element_type=F32) + b_ref[...]
    o_ref[...] = p[:, :D_MODEL] * _sigmoid(p[:, D_MODEL:])


def _glu(h, g, w, b, tm):
    rows = h.shape[0]
    row = pl.BlockSpec((tm, D_MODEL), lambda i: (i, 0))
    return pl.pallas_call(
        _glu_kernel,
        grid=(rows // tm,),
        in_specs=[row, _const_spec((1, D_MODEL)), _const_spec((D_MODEL, 2 * D_MODEL)),
                  _const_spec((1, 2 * D_MODEL))],
        out_specs=row,
        out_shape=jax.ShapeDtypeStruct((rows, D_MODEL), F32),
        compiler_params=_params(1),
        name="glu",
    )(h, g, w, b)


def _post_odd_kernel(tiles_per_batch, h_ref, u_ref, halo_ref, dw_ref, db_ref, lg_ref, lb_ref, w2_ref, b2_ref,
                     g_ref, wu_ref, wd_ref, o_ref, buf, cv):
    tm = h_ref.shape[0]
    first = (pl.program_id(0) % tiles_per_batch) == 0
    row = lax.broadcasted_iota(jnp.int32, (tm, 1), 0)
    halo = jnp.where(first, 0.0, halo_ref[...])
    main = jnp.where(first & (row < FRONT_PAD), 0.0, u_ref[...])
    for j in range(D_MODEL // LANES):
        buf[j, 0:HALO, :] = halo[:, j * LANES:(j + 1) * LANES]
        buf[j, HALO:HALO + tm, :] = main[:, j * LANES:(j + 1) * LANES]

    def conv_rows(i, carry):
        r0 = pl.multiple_of(i * CONV_ROWS, CONV_ROWS)
        for j in range(D_MODEL // LANES):
            cols = slice(j * LANES, (j + 1) * LANES)
            acc = jnp.broadcast_to(db_ref[:, cols], (CONV_ROWS, LANES))
            for k in range(CONF_KERNEL):
                off = HALO - (CONF_KERNEL - 1) + k
                acc = acc + dw_ref[k:k + 1, cols] * buf[j, pl.ds(r0 + off, CONV_ROWS), :]
            cv[pl.ds(r0, CONV_ROWS), cols] = acc
        return carry

    lax.fori_loop(0, tm // CONV_ROWS, conv_rows, 0)
    c = cv[...]
    mu = jnp.mean(c, axis=-1, keepdims=True)
    d = c - mu
    var = jnp.mean(d * d, axis=-1, keepdims=True)
    y = _silu(d * lax.rsqrt(var + LN_EPS) * lg_ref[...] + lb_ref[...])
    o_ref[...] = h_ref[...] + jnp.dot(y.astype(BF16), w2_ref[...], preferred_element_type=F32) + b2_ref[...]
    _mlp_tail(o_ref, g_ref, wu_ref, wd_ref)


def _post_odd(h, u, dw, db, lg, lb, w2, b2, g, wu, wd, tm, tiles_per_batch):
    rows = h.shape[0]
    row = pl.BlockSpec((tm, D_MODEL), lambda i: (i, 0))
    per = tm // HALO
    halo = pl.BlockSpec((HALO, D_MODEL), lambda i: (jnp.maximum(i * per - 1, 0), 0))
    vec = _const_spec((1, D_MODEL))
    return pl.pallas_call(
        functools.partial(_post_odd_kernel, tiles_per_batch),
        grid=(rows // tm,),
        in_specs=[row, row, halo, _const_spec((CONF_KERNEL, D_MODEL)), vec, vec, vec,
                  _const_spec((D_MODEL, D_MODEL)), vec, vec,
                  _const_spec((D_MODEL, D_FF)), _const_spec((D_FF, D_MODEL))],
        out_specs=row,
        out_shape=jax.ShapeDtypeStruct((rows, D_MODEL), F32),
        scratch_shapes=[pltpu.VMEM((D_MODEL // LANES, HALO + tm, LANES), F32), pltpu.VMEM((tm, D_MODEL), F32)],
        compiler_params=_params(1),
        name="post_odd",
    )(h, u, u, dw, db, lg, lb, w2, b2, g, wu, wd)


def _row_tile(blocks_per_batch):
    for k in (5, 4, 3, 2, 1):
        if blocks_per_batch % k == 0:
            return k * BLOCK


def _pair_perm():
    cols = []
    for g in range(ATT_GROUP):
        for kv in range(ATT_KV_HEADS):
            head = kv * ATT_GROUP + g
            cols.extend(range(head * HEAD_DIM, (head + 1) * HEAD_DIM))
    return jnp.asarray(cols, dtype=jnp.int32)


def _pad_lanes(v, width=LANES):
    return jnp.pad(v, (0, width - v.shape[0]))[None, :]


def kernel(x, meta_tokens, mix_norm_even, w_in, ssm_conv_w, ssm_conv_b, dt_bias, a_log, d_skip, ssm_norm_w, q_norm, k_norm, sinks, w_out, mix_norm_odd, pw1_w, pw1_b, dw_w, dw_b, ln_g, ln_b, pw2_w, pw2_b, mlp_norm, w_up, w_down):
    bsz, seq, _ = x.shape
    assert seq % BLOCK == 0
    lp = seq + BLOCK
    nb = lp // BLOCK
    tm = _row_tile(nb)
    rows = bsz * lp
    depth = mlp_norm.shape[0]

    meta = jnp.broadcast_to(meta_tokens[None].astype(x.dtype), (bsz, N_META, D_MODEL))
    h = jnp.concatenate([jnp.zeros((bsz, FRONT_PAD, D_MODEL), x.dtype), meta, x], axis=1).reshape(rows, D_MODEL)

    perm = _pair_perm()
    e_mat = (jnp.arange(LANES)[:, None] == (jnp.arange(SSM_INNER)[None, :] // SSM_HEAD_DIM)).astype(BF16)
    s_q, s_k, s_v = Q_W, Q_W + KV_W, Q_W + 2 * KV_W
    s_z = s_v + SSM_INNER
    s_xbc = s_z + XBC_W

    for layer in range(depth):
        i = layer // 2
        wu = w_up[layer].astype(BF16)
        wd = w_down[layer].astype(BF16)
        mg = mlp_norm[layer][None, :]
        if layer % 2 == 0:
            wi = w_in[i]
            w_cat = jnp.concatenate(
                [wi[:, :s_q][:, perm], wi[:, s_q:s_xbc],
                 jnp.pad(wi[:, s_xbc:], ((0, 0), (0, LANES - SSM_HEADS)))], axis=1).astype(BF16)
            qg = jnp.tile(q_norm[i] * (HEAD_DIM ** -0.5), Q_W // HEAD_DIM)[None, :]
            kg = jnp.tile(k_norm[i], KV_W // HEAD_DIM)[None, :]
            q, kv, z, xbc, dt = _inproj(h, mix_norm_even[i][None, :], w_cat, qg, kg, tm)
            att = _attention(sinks[i], q.reshape(bsz, lp, Q_W), kv.reshape(bsz, lp, 2 * LANES))
            ssm = _ssd(xbc.reshape(bsz, lp, XBC_W), dt.reshape(bsz, lp, LANES), z.reshape(bsz, lp, SSM_INNER),
                       ssm_conv_w[i], ssm_conv_b[i][None, :], _pad_lanes(dt_bias[i]), _pad_lanes(a_log[i]),
                       jnp.repeat(d_skip[i], SSM_HEAD_DIM)[None, :], ssm_norm_w[i][None, :], e_mat)
            wo = w_out[i]
            wa = wo[:Q_W][perm].astype(BF16)
            ws = wo[Q_W:].astype(BF16)
            h = _post_even(h, att.reshape(rows, Q_W), ssm.reshape(rows, SSM_INNER), wa, ws, mg, wu, wd, tm)
        else:
            u = _glu(h, mix_norm_odd[i][None, :], pw1_w[i].astype(BF16), pw1_b[i][None, :], tm)
            h = _post_odd(h, u, dw_w[i], dw_b[i][None, :], ln_g[i][None, :], ln_b[i][None, :],
                          pw2_w[i].astype(BF16), pw2_b[i][None, :], mg, wu, wd, tm, lp // tm)
    return h.reshape(bsz, lp, D_MODEL)[:, BLOCK:]
```

```python
import functools

import jax
import jax.numpy as jnp
from jax import lax
from jax.experimental import pallas as pl
from jax.experimental.pallas import tpu as pltpu

D_MODEL = 1024
N_META = 16
BLOCK = 128
WINDOW = 128
ATT_HEADS = 8
ATT_KV_HEADS = 2
ATT_GROUP = ATT_HEADS // ATT_KV_HEADS
HEAD_DIM = 64
SSM_HEADS = 16
SSM_HEAD_DIM = 64
SSM_INNER = SSM_HEADS * SSM_HEAD_DIM
SSM_GROUPS = 2
SSM_STATE = 64
SSM_CONV = 4
CONF_KERNEL = 31
D_FF = 4 * D_MODEL
EPS = 1e-6
LN_EPS = 1e-5

Q_W = ATT_HEADS * HEAD_DIM
KV_W = ATT_KV_HEADS * HEAD_DIM
BC_W = SSM_GROUPS * SSM_STATE
XBC_W = SSM_INNER + 2 * BC_W
FRONT_PAD = BLOCK - N_META

LANES = 128
HALO = 32
CONV_ROWS = 128
FF_CHUNK = 1024
VMEM_LIMIT = 56 * 1024 * 1024
NEG = -1e30

BF16 = jnp.bfloat16
F32 = jnp.float32


def _const_spec(shape):
    zeros = (0,) * len(shape)
    return pl.BlockSpec(shape, lambda *_: zeros, pipeline_mode=pl.Buffered(1))


def _params(n_axes, parallel=True):
    sem = ("parallel" if parallel else "arbitrary",) * n_axes
    return pltpu.CompilerParams(dimension_semantics=sem, vmem_limit_bytes=VMEM_LIMIT)


def _rms(x, g):
    return x * lax.rsqrt(jnp.mean(x * x, axis=-1, keepdims=True) + EPS) * g


def _sigmoid(x):
    return 0.5 + 0.5 * jnp.tanh(0.5 * x)


def _silu(x):
    h = 0.5 * x
    return h + h * jnp.tanh(h)


def _head_rms(x, gain):
    lane = lax.broadcasted_iota(jnp.int32, (1, LANES), 1)
    low = lane < HEAD_DIM
    outs = []
    for t in range(x.shape[1] // LANES):
        xt = x[:, t * LANES:(t + 1) * LANES]
        sq = xt * xt
        lo = jnp.sum(jnp.where(low, sq, 0.0), axis=-1, keepdims=True)
        hi = jnp.sum(jnp.where(low, 0.0, sq), axis=-1, keepdims=True)
        r = jnp.where(low, lax.rsqrt(lo * (1.0 / HEAD_DIM) + EPS), lax.rsqrt(hi * (1.0 / HEAD_DIM) + EPS))
        outs.append(xt * r * gain[:, t * LANES:(t + 1) * LANES])
    return outs[0] if len(outs) == 1 else jnp.concatenate(outs, axis=1)


def _split3(x):
    h1 = x.astype(BF16)
    r1 = x - h1.astype(F32)
    h2 = r1.astype(BF16)
    h3 = (r1 - h2.astype(F32)).astype(BF16)
    return h1, h2, h3


IN_COLS = Q_W + 2 * LANES + SSM_INNER + XBC_W + LANES


def _stream(src, tm, tiles_per_batch):
    if src[0] == "h":
        return [pl.BlockSpec((tm, D_MODEL), lambda i: (i, 0))], [src[1]], lambda refs: refs[0][...]
    _, x, meta = src
    assert tiles_per_batch >= 2
    seq = tiles_per_batch * tm - BLOCK

    def x_index(i):
        start = (i // tiles_per_batch) * seq + jnp.maximum((i % tiles_per_batch) * tm - BLOCK, 0)
        return pl.multiple_of(start, BLOCK), 0

    def load(refs):
        x_ref, meta_ref = refs
        xb = x_ref[...]
        front = jnp.concatenate([jnp.zeros((FRONT_PAD, D_MODEL), F32), meta_ref[...]], axis=0)
        first = jnp.concatenate([front, xb[:tm - BLOCK]], axis=0)
        return jnp.where(pl.program_id(0) % tiles_per_batch == 0, first, xb)

    specs = [pl.BlockSpec((pl.Element(tm), pl.Element(D_MODEL)), x_index), _const_spec((N_META, D_MODEL))]
    return specs, [x, meta], load


def _inproj_kernel(load, n_src, *refs):
    g_ref, w_ref, qg_ref, kg_ref, q_ref, kv_ref, z_ref, xbc_ref, dt_ref = refs[n_src:]
    u = _rms(load(refs[:n_src]), g_ref[...]).astype(BF16)
    p = jnp.dot(u, w_ref[...], preferred_element_type=F32)
    o = 0
    q = _head_rms(p[:, o:o + Q_W], qg_ref[...])
    o += Q_W
    k = _head_rms(p[:, o:o + LANES], kg_ref[...])
    o += LANES
    v = p[:, o:o + LANES]
    o += LANES
    q_ref[...] = q.astype(BF16)
    kv_ref[...] = jnp.concatenate([k, v], axis=1).astype(BF16)
    z_ref[...] = p[:, o:o + SSM_INNER]
    o += SSM_INNER
    xbc_ref[...] = p[:, o:o + XBC_W]
    o += XBC_W
    dt_ref[...] = p[:, o:o + LANES]


def _inproj(src, rows, g, w, qg, kg, tm, tiles_per_batch):
    row = lambda width: pl.BlockSpec((tm, width), lambda i: (i, 0))
    src_specs, src_args, load = _stream(src, tm, tiles_per_batch)
    return pl.pallas_call(
        functools.partial(_inproj_kernel, load, len(src_args)),
        grid=(rows // tm,),
        in_specs=src_specs + [_const_spec((1, D_MODEL)), _const_spec((D_MODEL, IN_COLS)),
                              _const_spec((1, Q_W)), _const_spec((1, LANES))],
        out_specs=[row(Q_W), row(2 * LANES), row(SSM_INNER), row(XBC_W), row(LANES)],
        out_shape=[jax.ShapeDtypeStruct((rows, Q_W), BF16), jax.ShapeDtypeStruct((rows, 2 * LANES), BF16),
                   jax.ShapeDtypeStruct((rows, SSM_INNER), F32), jax.ShapeDtypeStruct((rows, XBC_W), F32),
                   jax.ShapeDtypeStruct((rows, LANES), F32)],
        compiler_params=_params(1),
        name="inproj",
    )(*src_args, g, w, qg, kg)


N_BIAS = 3


def _attn_kernel(sink_ref, q_ref, kv_ref, kvp_ref, kvm_ref, o_ref, bias_ref):
    t = pl.program_id(1)
    n_blocks = q_ref.shape[1] // BLOCK
    nq = ATT_GROUP * BLOCK
    nk = 2 * BLOCK + N_META
    col = lax.broadcasted_iota(jnp.int32, (1, nq), 1)
    grp = col // BLOCK

    @pl.when(t == 0)
    def _():
        for n in range(N_BIAS):
            q_pos = n * BLOCK + (col % BLOCK) - FRONT_PAD
            k_pos = (n - 1) * BLOCK + lax.broadcasted_iota(jnp.int32, (2 * BLOCK, 1), 0) - FRONT_PAD
            dist = q_pos - k_pos
            valid = (k_pos >= N_META) & (dist >= 0) & (dist < WINDOW)
            pen = jnp.abs(dist).astype(F32)
            dist_m = q_pos - lax.broadcasted_iota(jnp.int32, (N_META, 1), 0)
            valid_m = dist_m >= 0
            pen_m = jnp.minimum(jnp.abs(dist_m), WINDOW).astype(F32)
            for kv in range(ATT_KV_HEADS):
                slope = jnp.zeros((1, nq), F32)
                for g in range(ATT_GROUP):
                    slope = jnp.where(grp == g, 2.0 ** (-8.0 * (kv * ATT_GROUP + g + 1) / ATT_HEADS), slope)
                bias_ref[n, kv, 0:2 * BLOCK, :] = jnp.where(valid, -slope * pen, NEG)
                bias_ref[n, kv, 2 * BLOCK:nk, :] = jnp.where(valid_m, -slope * pen_m, NEG)

    lane = lax.broadcasted_iota(jnp.int32, (1, LANES), 1)
    sub = lax.broadcasted_iota(jnp.int32, (LANES, 1), 0)
    nt = (((1,), (1,)), ((), ()))
    halves = [lane < HEAD_DIM, lane >= HEAD_DIM]
    sinks = []
    for kv in range(ATT_KV_HEADS):
        sink = jnp.zeros((1, nq), F32)
        for g in range(ATT_GROUP):
            sink = jnp.where(grp == g, sink_ref[kv * ATT_GROUP + g], sink)
        sinks.append(sink)

    def transposed(v):
        return v.astype(F32).T.astype(BF16)

    kvm = kvm_ref[0, FRONT_PAD:BLOCK, :]
    k_meta = kvm[:, :LANES]
    vt_meta = transposed(kvm[:, LANES:])
    kv_prev = kvp_ref[0]
    k_prev = kv_prev[:, :LANES]
    vt_prev = transposed(kv_prev[:, LANES:])
    for b in range(n_blocks):
        rows = slice(b * BLOCK, (b + 1) * BLOCK)
        table = (N_BIAS - 1) if b >= N_BIAS - 1 else jnp.where(t == 0, b, N_BIAS - 1)
        q = q_ref[0, rows, :]
        qs = jnp.concatenate([q[:, g * LANES:(g + 1) * LANES] for g in range(ATT_GROUP)], axis=0)
        kv_cur = kv_ref[0, rows, :]
        k_cur = kv_cur[:, :LANES]
        vt_cur = transposed(kv_cur[:, LANES:])
        k_all = jnp.concatenate([k_prev, k_cur, k_meta], axis=0)
        v_t = jnp.concatenate([vt_prev, vt_cur, vt_meta], axis=1)
        out_t = None
        for kv in range(ATT_KV_HEADS):
            km = jnp.where(halves[kv], k_all, jnp.zeros_like(k_all))
            s = lax.dot_general(km, qs, nt, preferred_element_type=F32) + bias_ref[table, kv]
            mx = jnp.maximum(jnp.max(s, axis=0, keepdims=True), sinks[kv])
            e = jnp.exp(s - mx)
            den = jnp.sum(e, axis=0, keepdims=True) + jnp.exp(sinks[kv] - mx)
            o_t = jnp.dot(v_t, e.astype(BF16), preferred_element_type=F32) * (1.0 / den)
            out_t = o_t if out_t is None else jnp.where(sub < HEAD_DIM, out_t, o_t)
        o_ref[0, rows, :] = jnp.concatenate([out_t[:, g * BLOCK:(g + 1) * BLOCK].T for g in range(ATT_GROUP)],
                                            axis=1).astype(BF16)
        k_prev, vt_prev = k_cur, vt_cur


def _attention(sinks, q, kv, tm):
    bsz, lp, _ = q.shape
    per = tm // BLOCK
    return pl.pallas_call(
        _attn_kernel,
        grid=(bsz, lp // tm),
        in_specs=[pl.BlockSpec(memory_space=pltpu.SMEM),
                  pl.BlockSpec((1, tm, Q_W), lambda b, t: (b, t, 0)),
                  pl.BlockSpec((1, tm, 2 * LANES), lambda b, t: (b, t, 0)),
                  pl.BlockSpec((1, BLOCK, 2 * LANES), lambda b, t: (b, jnp.maximum(t * per - 1, 0), 0)),
                  pl.BlockSpec((1, BLOCK, 2 * LANES), lambda b, t: (b, 0, 0))],
        out_specs=pl.BlockSpec((1, tm, Q_W), lambda b, t: (b, t, 0)),
        out_shape=jax.ShapeDtypeStruct((bsz, lp, Q_W), BF16),
        scratch_shapes=[pltpu.VMEM((N_BIAS, ATT_KV_HEADS, 2 * BLOCK + N_META, ATT_GROUP * BLOCK), F32)],
        compiler_params=pltpu.CompilerParams(dimension_semantics=("parallel", "arbitrary"),
                                             vmem_limit_bytes=VMEM_LIMIT),
        name="attention",
    )(sinks, q, kv, kv, kv)


def _ssd_kernel(xbc_ref, dt_ref, z_ref, cw_ref, cb_ref, dtb_ref, alog_ref, dsk_ref, nw_ref, e_ref,
                o_ref, state, xpad):
    c = pl.program_id(1)
    n_chunks = xbc_ref.shape[1] // BLOCK
    n_tiles = XBC_W // LANES

    for j in range(n_tiles):
        xpad[j, 8:, :] = xbc_ref[0, :, j * LANES:(j + 1) * LANES]

    @pl.when(c == 0)
    def _():
        state[...] = jnp.zeros_like(state)
        for j in range(n_tiles):
            xpad[j, 0:8 + FRONT_PAD, :] = jnp.zeros((8 + FRONT_PAD, LANES), F32)

    ri = lax.broadcasted_iota(jnp.int32, (BLOCK, BLOCK), 0)
    ci = lax.broadcasted_iota(jnp.int32, (BLOCK, BLOCK), 1)
    tril = ri >= ci
    tril_b = jnp.where(tril, 1.0, 0.0).astype(BF16)
    lane = lax.broadcasted_iota(jnp.int32, (1, LANES), 1)
    low = lane < SSM_STATE
    row = lax.broadcasted_iota(jnp.int32, (BLOCK, 1), 0)
    sr = lax.broadcasted_iota(jnp.int32, (BLOCK, SSM_INNER), 0)
    sc = lax.broadcasted_iota(jnp.int32, (BLOCK, SSM_INNER), 1)
    same_group = (sr < SSM_STATE) == (sc < SSM_INNER // SSM_GROUPS)
    nt = (((1,), (1,)), ((), ()))
    hpg = SSM_HEADS // SSM_GROUPS
    gw = SSM_INNER // SSM_GROUPS
    a = -jnp.exp(alog_ref[...])
    e_mat = e_ref[...]
    st = state[...]

    for k in range(n_chunks):
        r0 = 8 + k * BLOCK
        rows = slice(k * BLOCK, (k + 1) * BLOCK)
        xa = []
        for j in range(n_tiles):
            cols = slice(j * LANES, (j + 1) * LANES)
            acc = cb_ref[:, cols]
            for s in range(SSM_CONV):
                acc = acc + cw_ref[SSM_CONV - 1 - s:SSM_CONV - s, cols] * xpad[j, r0 - s:r0 - s + BLOCK, :]
            xa.append(_silu(acc))
        xs = jnp.concatenate(xa[:SSM_INNER // LANES], axis=1)
        bm = xa[SSM_INNER // LANES]
        cm = xa[SSM_INNER // LANES + 1]

        dtr = dt_ref[0, rows, :] + dtb_ref[...]
        dt = jnp.maximum(dtr, 0.0) + jnp.log1p(jnp.exp(-jnp.abs(dtr)))
        if k == 0:
            dt = jnp.where((c > 0) | (row >= FRONT_PAD), dt, 0.0)
        adt = dt * a
        a_cs = sum(jnp.dot(tril_b, t, preferred_element_type=F32) for t in _split3(adt))
        a_last = a_cs[BLOCK - 1:BLOCK, :]
        wx = jnp.exp(a_last - a_cs) * dt
        ea = jnp.exp(a_cs)
        both = jnp.concatenate([wx, ea], axis=0)
        both_e = sum(jnp.dot(t, e_mat, preferred_element_type=F32) for t in _split3(both))
        wx_e = both_e[:BLOCK]
        ea_e = both_e[BLOCK:]
        decay_e = ea_e[BLOCK - 1:BLOCK, :]
        a_cs_t = a_cs.T
        dt_t = dt.T

        cm_b = cm.astype(BF16)
        bm_b = bm.astype(BF16)
        zero_b = jnp.zeros_like(cm_b)
        cb = [lax.dot_general(jnp.where(low, cm_b, zero_b), bm_b, nt, preferred_element_type=F32),
              lax.dot_general(jnp.where(low, zero_b, cm_b), bm_b, nt, preferred_element_type=F32)]

        y_tiles = []
        for j in range(SSM_HEADS // 2):
            xp = xa[j].astype(BF16)
            zx = jnp.zeros_like(xp)
            rhs = jnp.concatenate([jnp.where(low, xp, zx), jnp.where(low, zx, xp)], axis=0)
            ws = []
            for hd in (2 * j, 2 * j + 1):
                seg = a_cs[:, hd:hd + 1] - a_cs_t[hd:hd + 1, :]
                decay = jnp.exp(jnp.where(tril, seg, NEG))
                ws.append((cb[hd // hpg] * decay * dt_t[hd:hd + 1, :]).astype(BF16))
            y_tiles.append(jnp.dot(jnp.concatenate(ws, axis=1), rhs, preferred_element_type=F32))
        y = jnp.concatenate(y_tiles, axis=1)

        y = y + jnp.dot(cm_b, st.astype(BF16), preferred_element_type=F32) * ea_e
        upd = jnp.dot(bm.T.astype(BF16), (xs * wx_e).astype(BF16), preferred_element_type=F32)
        st = st * decay_e + jnp.where(same_group, upd, 0.0)

        y = y + xs * dsk_ref[...]
        y = y * _silu(z_ref[0, rows, :])
        outs = []
        for g in range(SSM_GROUPS):
            yg = y[:, g * gw:(g + 1) * gw]
            outs.append(yg * lax.rsqrt(jnp.mean(yg * yg, axis=-1, keepdims=True) + EPS))
        o_ref[0, rows, :] = (jnp.concatenate(outs, axis=1) * nw_ref[...]).astype(BF16)

    state[...] = st
    for j in range(n_tiles):
        xpad[j, 0:8, :] = xpad[j, n_chunks * BLOCK:n_chunks * BLOCK + 8, :]


def _ssd(xbc, dt, z, cw, cb, dtb, alog, dsk, nw, e_mat, tm):
    bsz, lp, _ = xbc.shape
    blk = lambda width: pl.BlockSpec((1, tm, width), lambda b, c: (b, c, 0))
    return pl.pallas_call(
        _ssd_kernel,
        grid=(bsz, lp // tm),
        in_specs=[blk(XBC_W), blk(LANES), blk(SSM_INNER),
                  _const_spec((SSM_CONV, XBC_W)), _const_spec((1, XBC_W)), _const_spec((1, LANES)),
                  _const_spec((1, LANES)), _const_spec((1, SSM_INNER)), _const_spec((1, SSM_INNER)),
                  _const_spec((LANES, SSM_INNER))],
        out_specs=blk(SSM_INNER),
        out_shape=jax.ShapeDtypeStruct((bsz, lp, SSM_INNER), BF16),
        scratch_shapes=[pltpu.VMEM((BLOCK, SSM_INNER), F32), pltpu.VMEM((XBC_W // LANES, 8 + tm, LANES), F32)],
        compiler_params=pltpu.CompilerParams(dimension_semantics=("parallel", "arbitrary"),
                                             vmem_limit_bytes=VMEM_LIMIT),
        name="ssd",
    )(xbc, dt, z, cw, cb, dtb, alog, dsk, nw, e_mat)


def _mlp_tail(o_ref, g_ref, wu_ref, wd_ref):
    x = o_ref[...]
    u = _rms(x, g_ref[...]).astype(BF16)
    acc = None
    for c in range(D_FF // FF_CHUNK):
        a = jnp.dot(u, wu_ref[:, c * FF_CHUNK:(c + 1) * FF_CHUNK], preferred_element_type=F32)
        a = jnp.maximum(a, 0.0)
        a = (a * a).astype(BF16)
        d = jnp.dot(a, wd_ref[c * FF_CHUNK:(c + 1) * FF_CHUNK, :], preferred_element_type=F32)
        acc = d if acc is None else acc + d
    o_ref[...] = x + acc


def _post_even_kernel(load, n_src, *refs):
    att_ref, ssm_ref, wa_ref, ws_ref, g_ref, wu_ref, wd_ref, o_ref = refs[n_src:]
    o_ref[...] = (load(refs[:n_src]) + jnp.dot(att_ref[...], wa_ref[...], preferred_element_type=F32)
                  + jnp.dot(ssm_ref[...], ws_ref[...], preferred_element_type=F32))
    _mlp_tail(o_ref, g_ref, wu_ref, wd_ref)


def _post_even(src, att, ssm, wa, ws, g, wu, wd, tm, tiles_per_batch):
    rows = att.shape[0]
    row = lambda width: pl.BlockSpec((tm, width), lambda i: (i, 0))
    src_specs, src_args, load = _stream(src, tm, tiles_per_batch)
    return pl.pallas_call(
        functools.partial(_post_even_kernel, load, len(src_args)),
        grid=(rows // tm,),
        in_specs=src_specs + [row(Q_W), row(SSM_INNER), _const_spec((Q_W, D_MODEL)),
                              _const_spec((SSM_INNER, D_MODEL)), _const_spec((1, D_MODEL)),
                              _const_spec((D_MODEL, D_FF)), _const_spec((D_FF, D_MODEL))],
        out_specs=row(D_MODEL),
        out_shape=jax.ShapeDtypeStruct((rows, D_MODEL), F32),
        compiler_params=_params(1),
        name="post_even",
    )(*src_args, att, ssm, wa, ws, g, wu, wd)


def _glu_kernel(h_ref, g_ref, w_ref, b_ref, o_ref):
    u = _rms(h_ref[...], g_ref[...]).astype(BF16)
    p = jnp.dot(u, w_ref[...], preferred_element_type=F32) + b_ref[...]
    o_ref[...] = p[:, :D_MODEL] * _sigmoid(p[:, D_MODEL:])


def _glu(h, g, w, b, tm):
    rows = h.shape[0]
    row = pl.BlockSpec((tm, D_MODEL), lambda i: (i, 0))
    return pl.pallas_call(
        _glu_kernel,
        grid=(rows // tm,),
        in_specs=[row, _const_spec((1, D_MODEL)), _const_spec((D_MODEL, 2 * D_MODEL)),
                  _const_spec((1, 2 * D_MODEL))],
        out_specs=row,
        out_shape=jax.ShapeDtypeStruct((rows, D_MODEL), F32),
        compiler_params=_params(1),
        name="glu",
    )(h, g, w, b)


def _conv_fill(main, halo, buf):
    tm = main.shape[0]
    for j in range(D_MODEL // LANES):
        buf[j, 0:HALO, :] = halo[:, j * LANES:(j + 1) * LANES]
        buf[j, HALO:HALO + tm, :] = main[:, j * LANES:(j + 1) * LANES]


def _conv_unit(dw_ref, db_ref, buf, cv, r0, j):
    cols = slice(j * LANES, (j + 1) * LANES)
    acc = jnp.broadcast_to(db_ref[:, cols], (CONV_ROWS, LANES))
    for k in range(CONF_KERNEL):
        off = HALO - (CONF_KERNEL - 1) + k
        acc = acc + dw_ref[k:k + 1, cols] * buf[j, pl.ds(r0 + off, CONV_ROWS), :]
    cv[pl.ds(r0, CONV_ROWS), cols] = acc


def _post_odd_kernel(tiles_per_batch, final, h_ref, u_ref, halo_ref, dw_ref, db_ref, lg_ref, lb_ref, w2_ref,
                     b2_ref, g_ref, wu_ref, wd_ref, o_ref, buf, cv):
    tm = h_ref.shape[0]
    first = (pl.program_id(0) % tiles_per_batch) == 0
    if final:
        hrow = lax.broadcasted_iota(jnp.int32, (HALO, 1), 0)
        halo = jnp.where(first & (hrow < HALO - N_META), 0.0, halo_ref[...])
        main = u_ref[...]
    else:
        row = lax.broadcasted_iota(jnp.int32, (tm, 1), 0)
        halo = jnp.where(first, 0.0, halo_ref[...])
        main = jnp.where(first & (row < FRONT_PAD), 0.0, u_ref[...])
    _conv_fill(main, halo, buf)

    def body(r, carry):
        for j in range(D_MODEL // LANES):
            _conv_unit(dw_ref, db_ref, buf, cv, pl.multiple_of(r * CONV_ROWS, CONV_ROWS), j)
        return carry

    lax.fori_loop(0, tm // CONV_ROWS, body, 0)
    c = cv[...]
    mu = jnp.mean(c, axis=-1, keepdims=True)
    d = c - mu
    var = jnp.mean(d * d, axis=-1, keepdims=True)
    y = _silu(d * lax.rsqrt(var + LN_EPS) * lg_ref[...] + lb_ref[...])
    o_ref[...] = h_ref[...] + jnp.dot(y.astype(BF16), w2_ref[...], preferred_element_type=F32) + b2_ref[...]
    _mlp_tail(o_ref, g_ref, wu_ref, wd_ref)


def _post_odd(h, u, dw, db, lg, lb, w2, b2, g, wu, wd, tm, tiles_per_batch, final_lp=None):
    out_rows = h.shape[0] if final_lp is None else tiles_per_batch * tm * (h.shape[0] // final_lp)
    row = pl.BlockSpec((tm, D_MODEL), lambda i: (i, 0))
    if final_lp is None:
        per = tm // HALO
        src = row
        halo = pl.BlockSpec((HALO, D_MODEL), lambda i: (jnp.maximum(i * per - 1, 0), 0))
    else:
        def start(i):
            return (i // tiles_per_batch) * final_lp + BLOCK + (i % tiles_per_batch) * tm
        src = pl.BlockSpec((pl.Element(tm), pl.Element(D_MODEL)), lambda i: (pl.multiple_of(start(i), BLOCK), 0))
        halo = pl.BlockSpec((pl.Element(HALO), pl.Element(D_MODEL)),
                            lambda i: (pl.multiple_of(start(i) - HALO, HALO), 0))
    vec = _const_spec((1, D_MODEL))
    return pl.pallas_call(
        functools.partial(_post_odd_kernel, tiles_per_batch, final_lp is not None),
        grid=(out_rows // tm,),
        in_specs=[src, src, halo, _const_spec((CONF_KERNEL, D_MODEL)), vec, vec, vec,
                  _const_spec((D_MODEL, D_MODEL)), vec, vec,
                  _const_spec((D_MODEL, D_FF)), _const_spec((D_FF, D_MODEL))],
        out_specs=row,
        out_shape=jax.ShapeDtypeStruct((out_rows, D_MODEL), F32),
        scratch_shapes=[pltpu.VMEM((D_MODEL // LANES, HALO + tm, LANES), F32), pltpu.VMEM((tm, D_MODEL), F32)],
        compiler_params=_params(1),
        name="post_odd",
    )(h, u, u, dw, db, lg, lb, w2, b2, g, wu, wd)


def _row_tile(blocks_per_batch):
    for k in (5, 4, 3, 2, 1):
        if blocks_per_batch % k == 0:
            return k * BLOCK


def _pair_perm():
    cols = []
    for g in range(ATT_GROUP):
        for kv in range(ATT_KV_HEADS):
            head = kv * ATT_GROUP + g
            cols.extend(range(head * HEAD_DIM, (head + 1) * HEAD_DIM))
    return jnp.asarray(cols, dtype=jnp.int32)


def _pad_lanes(v, width=LANES):
    return jnp.pad(v, (0, width - v.shape[0]))[None, :]


def kernel(x, meta_tokens, mix_norm_even, w_in, ssm_conv_w, ssm_conv_b, dt_bias, a_log, d_skip, ssm_norm_w, q_norm, k_norm, sinks, w_out, mix_norm_odd, pw1_w, pw1_b, dw_w, dw_b, ln_g, ln_b, pw2_w, pw2_b, mlp_norm, w_up, w_down):
    bsz, seq, _ = x.shape
    assert seq % BLOCK == 0
    lp = seq + BLOCK
    nb = lp // BLOCK
    tm = _row_tile(nb)
    tpb = lp // tm
    rows = bsz * lp
    depth = mlp_norm.shape[0]
    assert depth % 2 == 0
    final_tm = next(k * BLOCK for k in (4, 3, 2, 1) if seq % (k * BLOCK) == 0)

    src = ("x", x.reshape(bsz * seq, D_MODEL), meta_tokens.astype(x.dtype))

    perm = _pair_perm()
    e_mat = (jnp.arange(LANES)[:, None] == (jnp.arange(SSM_INNER)[None, :] // SSM_HEAD_DIM)).astype(BF16)
    s_q, s_k, s_v = Q_W, Q_W + KV_W, Q_W + 2 * KV_W
    s_z = s_v + SSM_INNER
    s_xbc = s_z + XBC_W

    for layer in range(depth):
        i = layer // 2
        wu = w_up[layer].astype(BF16)
        wd = w_down[layer].astype(BF16)
        mg = mlp_norm[layer][None, :]
        if layer % 2 == 0:
            wi = w_in[i]
            w_cat = jnp.concatenate(
                [wi[:, :s_q][:, perm], wi[:, s_q:s_xbc],
                 jnp.pad(wi[:, s_xbc:], ((0, 0), (0, LANES - SSM_HEADS)))], axis=1).astype(BF16)
            qg = jnp.tile(q_norm[i] * (HEAD_DIM ** -0.5), Q_W // HEAD_DIM)[None, :]
            kg = jnp.tile(k_norm[i], KV_W // HEAD_DIM)[None, :]
            q, kv, z, xbc, dt = _inproj(src, rows, mix_norm_even[i][None, :], w_cat, qg, kg, tm, tpb)
            att = _attention(sinks[i], q.reshape(bsz, lp, Q_W), kv.reshape(bsz, lp, 2 * LANES), tm)
            ssm = _ssd(xbc.reshape(bsz, lp, XBC_W), dt.reshape(bsz, lp, LANES), z.reshape(bsz, lp, SSM_INNER),
                       ssm_conv_w[i], ssm_conv_b[i][None, :], _pad_lanes(dt_bias[i]), _pad_lanes(a_log[i]),
                       jnp.repeat(d_skip[i], SSM_HEAD_DIM)[None, :], ssm_norm_w[i][None, :], e_mat, tm)
            wo = w_out[i]
            wa = wo[:Q_W][perm].astype(BF16)
            ws = wo[Q_W:].astype(BF16)
            h = _post_even(src, att.reshape(rows, Q_W), ssm.reshape(rows, SSM_INNER), wa, ws, mg, wu, wd, tm, tpb)
        else:
            u = _glu(h, mix_norm_odd[i][None, :], pw1_w[i].astype(BF16), pw1_b[i][None, :], tm)
            conv = (dw_w[i], dw_b[i][None, :], ln_g[i][None, :], ln_b[i][None, :],
                    pw2_w[i].astype(BF16), pw2_b[i][None, :], mg, wu, wd)
            if layer == depth - 1:
                h = _post_odd(h, u, *conv, final_tm, seq // final_tm, final_lp=lp)
            else:
                h = _post_odd(h, u, *conv, tm, tpb)
        src = ("h", h)
    return h.reshape(bsz, seq, D_MODEL)
```

```python
import functools

import jax
import jax.numpy as jnp
from jax import lax
from jax.experimental import pallas as pl
from jax.experimental.pallas import tpu as pltpu

D_MODEL = 1024
N_META = 16
BLOCK = 128
WINDOW = 128
ATT_HEADS = 8
ATT_KV_HEADS = 2
ATT_GROUP = ATT_HEADS // ATT_KV_HEADS
HEAD_DIM = 64
SSM_HEADS = 16
SSM_HEAD_DIM = 64
SSM_INNER = SSM_HEADS * SSM_HEAD_DIM
SSM_GROUPS = 2
SSM_STATE = 64
SSM_CONV = 4
CONF_KERNEL = 31
D_FF = 4 * D_MODEL
EPS = 1e-6
LN_EPS = 1e-5

Q_W = ATT_HEADS * HEAD_DIM
KV_W = ATT_KV_HEADS * HEAD_DIM
BC_W = SSM_GROUPS * SSM_STATE
XBC_W = SSM_INNER + 2 * BC_W
FRONT_PAD = BLOCK - N_META

LANES = 128
HALO = 32
CONV_ROWS = 128
FF_CHUNK = 1024
VMEM_LIMIT = 56 * 1024 * 1024
NEG = -1e30

BF16 = jnp.bfloat16
F32 = jnp.float32


def _const_spec(shape):
    zeros = (0,) * len(shape)
    return pl.BlockSpec(shape, lambda *_: zeros, pipeline_mode=pl.Buffered(1))


def _params(n_axes, parallel=True):
    sem = ("parallel" if parallel else "arbitrary",) * n_axes
    return pltpu.CompilerParams(dimension_semantics=sem, vmem_limit_bytes=VMEM_LIMIT)


def _rms(x, g):
    return x * lax.rsqrt(jnp.mean(x * x, axis=-1, keepdims=True) + EPS) * g


def _sigmoid(x):
    return 0.5 + 0.5 * jnp.tanh(0.5 * x)


def _silu(x):
    h = 0.5 * x
    return h + h * jnp.tanh(h)


def _head_rms(x, gain):
    lane = lax.broadcasted_iota(jnp.int32, (1, LANES), 1)
    low = lane < HEAD_DIM
    outs = []
    for t in range(x.shape[1] // LANES):
        xt = x[:, t * LANES:(t + 1) * LANES]
        sq = xt * xt
        lo = jnp.sum(jnp.where(low, sq, 0.0), axis=-1, keepdims=True)
        hi = jnp.sum(jnp.where(low, 0.0, sq), axis=-1, keepdims=True)
        r = jnp.where(low, lax.rsqrt(lo * (1.0 / HEAD_DIM) + EPS), lax.rsqrt(hi * (1.0 / HEAD_DIM) + EPS))
        outs.append(xt * r * gain[:, t * LANES:(t + 1) * LANES])
    return outs[0] if len(outs) == 1 else jnp.concatenate(outs, axis=1)


def _split3(x):
    h1 = x.astype(BF16)
    r1 = x - h1.astype(F32)
    h2 = r1.astype(BF16)
    h3 = (r1 - h2.astype(F32)).astype(BF16)
    return h1, h2, h3


IN_COLS = Q_W + 2 * LANES + SSM_INNER + XBC_W + LANES


def _stream(src, tm, tiles_per_batch):
    if src[0] == "h":
        return [pl.BlockSpec((tm, D_MODEL), lambda i: (i, 0))], [src[1]], lambda refs: refs[0][...]
    _, x, meta = src
    assert tiles_per_batch >= 2
    seq = tiles_per_batch * tm - BLOCK

    def x_index(i):
        start = (i // tiles_per_batch) * seq + jnp.maximum((i % tiles_per_batch) * tm - BLOCK, 0)
        return pl.multiple_of(start, BLOCK), 0

    def load(refs):
        x_ref, meta_ref = refs
        xb = x_ref[...]
        front = jnp.concatenate([jnp.zeros((FRONT_PAD, D_MODEL), F32), meta_ref[...]], axis=0)
        first = jnp.concatenate([front, xb[:tm - BLOCK]], axis=0)
        return jnp.where(pl.program_id(0) % tiles_per_batch == 0, first, xb)

    specs = [pl.BlockSpec((pl.Element(tm), pl.Element(D_MODEL)), x_index), _const_spec((N_META, D_MODEL))]
    return specs, [x, meta], load


def _inproj_kernel(load, n_src, *refs):
    g_ref, w_ref, qg_ref, kg_ref, q_ref, kv_ref, z_ref, xbc_ref, dt_ref = refs[n_src:]
    u = _rms(load(refs[:n_src]), g_ref[...]).astype(BF16)
    p = jnp.dot(u, w_ref[...], preferred_element_type=F32)
    o = 0
    q = _head_rms(p[:, o:o + Q_W], qg_ref[...])
    o += Q_W
    k = _head_rms(p[:, o:o + LANES], kg_ref[...])
    o += LANES
    v = p[:, o:o + LANES]
    o += LANES
    q_ref[...] = q.astype(BF16)
    kv_ref[...] = jnp.concatenate([k, v], axis=1).astype(BF16)
    z_ref[...] = p[:, o:o + SSM_INNER]
    o += SSM_INNER
    xbc_ref[...] = p[:, o:o + XBC_W]
    o += XBC_W
    dt_ref[...] = p[:, o:o + LANES]


def _inproj(src, rows, g, w, qg, kg, tm, tiles_per_batch):
    row = lambda width: pl.BlockSpec((tm, width), lambda i: (i, 0))
    src_specs, src_args, load = _stream(src, tm, tiles_per_batch)
    return pl.pallas_call(
        functools.partial(_inproj_kernel, load, len(src_args)),
        grid=(rows // tm,),
        in_specs=src_specs + [_const_spec((1, D_MODEL)), _const_spec((D_MODEL, IN_COLS)),
                              _const_spec((1, Q_W)), _const_spec((1, LANES))],
        out_specs=[row(Q_W), row(2 * LANES), row(SSM_INNER), row(XBC_W), row(LANES)],
        out_shape=[jax.ShapeDtypeStruct((rows, Q_W), BF16), jax.ShapeDtypeStruct((rows, 2 * LANES), BF16),
                   jax.ShapeDtypeStruct((rows, SSM_INNER), F32), jax.ShapeDtypeStruct((rows, XBC_W), F32),
                   jax.ShapeDtypeStruct((rows, LANES), F32)],
        compiler_params=_params(1),
        name="inproj",
    )(*src_args, g, w, qg, kg)


N_BIAS = 3


def _attn_kernel(sink_ref, q_ref, kv_ref, kvp_ref, kvm_ref, o_ref, bias_ref):
    t = pl.program_id(1)
    n_blocks = q_ref.shape[1] // BLOCK
    nq = ATT_GROUP * BLOCK
    nk = 2 * BLOCK + N_META
    col = lax.broadcasted_iota(jnp.int32, (1, nq), 1)
    grp = col // BLOCK

    @pl.when(t == 0)
    def _():
        for n in range(N_BIAS):
            q_pos = n * BLOCK + (col % BLOCK) - FRONT_PAD
            k_pos = (n - 1) * BLOCK + lax.broadcasted_iota(jnp.int32, (2 * BLOCK, 1), 0) - FRONT_PAD
            dist = q_pos - k_pos
            valid = (k_pos >= N_META) & (dist >= 0) & (dist < WINDOW)
            pen = jnp.abs(dist).astype(F32)
            dist_m = q_pos - lax.broadcasted_iota(jnp.int32, (N_META, 1), 0)
            valid_m = dist_m >= 0
            pen_m = jnp.minimum(jnp.abs(dist_m), WINDOW).astype(F32)
            for kv in range(ATT_KV_HEADS):
                slope = jnp.zeros((1, nq), F32)
                for g in range(ATT_GROUP):
                    slope = jnp.where(grp == g, 2.0 ** (-8.0 * (kv * ATT_GROUP + g + 1) / ATT_HEADS), slope)
                bias_ref[n, kv, 0:2 * BLOCK, :] = jnp.where(valid, -slope * pen, NEG)
                bias_ref[n, kv, 2 * BLOCK:nk, :] = jnp.where(valid_m, -slope * pen_m, NEG)

    lane = lax.broadcasted_iota(jnp.int32, (1, LANES), 1)
    sub = lax.broadcasted_iota(jnp.int32, (LANES, 1), 0)
    nt = (((1,), (1,)), ((), ()))
    halves = [lane < HEAD_DIM, lane >= HEAD_DIM]
    sinks = []
    for kv in range(ATT_KV_HEADS):
        sink = jnp.zeros((1, nq), F32)
        for g in range(ATT_GROUP):
            sink = jnp.where(grp == g, sink_ref[kv * ATT_GROUP + g], sink)
        sinks.append(sink)

    def transposed(v):
        return v.astype(F32).T.astype(BF16)

    kvm = kvm_ref[0, FRONT_PAD:BLOCK, :]
    k_meta = kvm[:, :LANES]
    vt_meta = transposed(kvm[:, LANES:])
    kv_prev = kvp_ref[0]
    k_prev = kv_prev[:, :LANES]
    vt_prev = transposed(kv_prev[:, LANES:])
    for b in range(n_blocks):
        rows = slice(b * BLOCK, (b + 1) * BLOCK)
        table = (N_BIAS - 1) if b >= N_BIAS - 1 else jnp.where(t == 0, b, N_BIAS - 1)
        q = q_ref[0, rows, :]
        qs = jnp.concatenate([q[:, g * LANES:(g + 1) * LANES] for g in range(ATT_GROUP)], axis=0)
        kv_cur = kv_ref[0, rows, :]
        k_cur = kv_cur[:, :LANES]
        vt_cur = transposed(kv_cur[:, LANES:])
        k_all = jnp.concatenate([k_prev, k_cur, k_meta], axis=0)
        v_t = jnp.concatenate([vt_prev, vt_cur, vt_meta], axis=1)
        zk = jnp.zeros_like(k_all)
        km = jnp.concatenate([jnp.where(halves[0], k_all, zk), jnp.where(halves[1], k_all, zk)], axis=0)
        s_both = lax.dot_general(km, qs, nt, preferred_element_type=F32)
        out_t = None
        for kv in range(ATT_KV_HEADS):
            s = s_both[kv * nk:(kv + 1) * nk] + bias_ref[table, kv]
            mx = jnp.maximum(jnp.max(s, axis=0, keepdims=True), sinks[kv])
            e = jnp.exp(s - mx)
            den = jnp.sum(e, axis=0, keepdims=True) + jnp.exp(sinks[kv] - mx)
            o_t = jnp.dot(v_t, e.astype(BF16), preferred_element_type=F32) * (1.0 / den)
            out_t = o_t if out_t is None else jnp.where(sub < HEAD_DIM, out_t, o_t)
        o_ref[0, rows, :] = jnp.concatenate([out_t[:, g * BLOCK:(g + 1) * BLOCK].T for g in range(ATT_GROUP)],
                                            axis=1).astype(BF16)
        k_prev, vt_prev = k_cur, vt_cur


def _attention(sinks, q, kv, tm):
    bsz, lp, _ = q.shape
    per = tm // BLOCK
    return pl.pallas_call(
        _attn_kernel,
        grid=(bsz, lp // tm),
        in_specs=[pl.BlockSpec(memory_space=pltpu.SMEM),
                  pl.BlockSpec((1, tm, Q_W), lambda b, t: (b, t, 0)),
                  pl.BlockSpec((1, tm, 2 * LANES), lambda b, t: (b, t, 0)),
                  pl.BlockSpec((1, BLOCK, 2 * LANES), lambda b, t: (b, jnp.maximum(t * per - 1, 0), 0)),
                  pl.BlockSpec((1, BLOCK, 2 * LANES), lambda b, t: (b, 0, 0))],
        out_specs=pl.BlockSpec((1, tm, Q_W), lambda b, t: (b, t, 0)),
        out_shape=jax.ShapeDtypeStruct((bsz, lp, Q_W), BF16),
        scratch_shapes=[pltpu.VMEM((N_BIAS, ATT_KV_HEADS, 2 * BLOCK + N_META, ATT_GROUP * BLOCK), F32)],
        compiler_params=pltpu.CompilerParams(dimension_semantics=("parallel", "arbitrary"),
                                             vmem_limit_bytes=VMEM_LIMIT),
        name="attention",
    )(sinks, q, kv, kv, kv)


def _ssd_kernel(xbc_ref, dt_ref, z_ref, cw_ref, cb_ref, dtb_ref, alog_ref, dsk_ref, nw_ref, e_ref,
                o_ref, state, xpad):
    c = pl.program_id(1)
    n_chunks = xbc_ref.shape[1] // BLOCK
    n_tiles = XBC_W // LANES

    for j in range(n_tiles):
        xpad[j, 8:, :] = xbc_ref[0, :, j * LANES:(j + 1) * LANES]

    @pl.when(c == 0)
    def _():
        state[...] = jnp.zeros_like(state)
        for j in range(n_tiles):
            xpad[j, 0:8 + FRONT_PAD, :] = jnp.zeros((8 + FRONT_PAD, LANES), F32)

    ri = lax.broadcasted_iota(jnp.int32, (BLOCK, BLOCK), 0)
    ci = lax.broadcasted_iota(jnp.int32, (BLOCK, BLOCK), 1)
    tril = ri >= ci
    tril_b = jnp.where(tril, 1.0, 0.0).astype(BF16)
    lane = lax.broadcasted_iota(jnp.int32, (1, LANES), 1)
    low = lane < SSM_STATE
    head_lane = lane < SSM_HEADS
    first_group = lax.broadcasted_iota(jnp.int32, (BLOCK, 1), 0) < SSM_STATE
    nt = (((1,), (1,)), ((), ()))
    hpg = SSM_HEADS // SSM_GROUPS
    gw = SSM_INNER // SSM_GROUPS
    a = -jnp.exp(alog_ref[...])
    e_mat = e_ref[...]
    st = state[...]

    for k in range(n_chunks):
        r0 = 8 + k * BLOCK
        rows = slice(k * BLOCK, (k + 1) * BLOCK)
        xa = []
        for j in range(n_tiles):
            cols = slice(j * LANES, (j + 1) * LANES)
            acc = cb_ref[:, cols]
            for s in range(SSM_CONV):
                acc = acc + cw_ref[SSM_CONV - 1 - s:SSM_CONV - s, cols] * xpad[j, r0 - s:r0 - s + BLOCK, :]
            xa.append(_silu(acc))
        xs = jnp.concatenate(xa[:SSM_INNER // LANES], axis=1)
        bm = xa[SSM_INNER // LANES]
        cm = xa[SSM_INNER // LANES + 1]

        dtr_t = (dt_ref[0, rows, :] + dtb_ref[...]).T[0:SSM_HEADS, :]
        dt_t = jnp.maximum(dtr_t, 0.0) + jnp.log1p(jnp.exp(-jnp.abs(dtr_t)))
        if k == 0:
            dt_t = jnp.where((c > 0) | (lane >= FRONT_PAD), dt_t, 0.0)
        dt = jnp.concatenate([dt_t, jnp.zeros((BLOCK - SSM_HEADS, BLOCK), F32)], axis=0).T
        adt = dt * a
        a_cs = sum(jnp.dot(tril_b, t, preferred_element_type=F32) for t in _split3(adt))
        a_last = a_cs[BLOCK - 1:BLOCK, :]
        wx = jnp.exp(a_last - a_cs) * dt
        ea = jnp.where(head_lane, jnp.exp(a_cs), 0.0)
        h1, h2, h3 = _split3(jnp.concatenate([wx, ea], axis=0))
        packed = (h1.astype(F32) + pltpu.roll(h2.astype(F32), SSM_HEADS, axis=1)
                  + pltpu.roll(h3.astype(F32), 2 * SSM_HEADS, axis=1)).astype(BF16)
        both_e = jnp.dot(packed, e_mat, preferred_element_type=F32)
        wx_e = both_e[:BLOCK]
        ea_e = both_e[BLOCK:]
        decay_e = ea_e[BLOCK - 1:BLOCK, :]
        a_cs_t = a_cs.T

        cm_b = cm.astype(BF16)
        bm_b = bm.astype(BF16)
        zero_b = jnp.zeros_like(cm_b)
        cm_split = jnp.concatenate([jnp.where(low, cm_b, zero_b), jnp.where(low, zero_b, cm_b)], axis=0)
        cb_both = lax.dot_general(cm_split, bm_b, nt, preferred_element_type=F32)
        cb = [cb_both[:BLOCK], cb_both[BLOCK:]]

        y_tiles = []
        for j in range(SSM_HEADS // 2):
            xp = xa[j].astype(BF16)
            zx = jnp.zeros_like(xp)
            rhs = jnp.concatenate([jnp.where(low, xp, zx), jnp.where(low, zx, xp)], axis=0)
            ws = []
            for hd in (2 * j, 2 * j + 1):
                seg = a_cs[:, hd:hd + 1] - a_cs_t[hd:hd + 1, :]
                decay = jnp.exp(jnp.where(tril, seg, NEG))
                ws.append((cb[hd // hpg] * decay * dt_t[hd:hd + 1, :]).astype(BF16))
            y_tiles.append(jnp.dot(jnp.concatenate(ws, axis=1), rhs, preferred_element_type=F32))
        y = jnp.concatenate(y_tiles, axis=1)

        y = y + jnp.dot(cm_b, st.astype(BF16), preferred_element_type=F32) * ea_e
        bm_t = bm.T.astype(BF16)
        zero_t = jnp.zeros_like(bm_t)
        xw = (xs * wx_e).astype(BF16)
        upd = jnp.concatenate(
            [jnp.dot(jnp.where(first_group, bm_t, zero_t), xw[:, :gw], preferred_element_type=F32),
             jnp.dot(jnp.where(first_group, zero_t, bm_t), xw[:, gw:], preferred_element_type=F32)], axis=1)
        st = st * decay_e + upd

        y = y + xs * dsk_ref[...]
        y = y * _silu(z_ref[0, rows, :])
        outs = []
        for g in range(SSM_GROUPS):
            yg = y[:, g * gw:(g + 1) * gw]
            outs.append(yg * lax.rsqrt(jnp.mean(yg * yg, axis=-1, keepdims=True) + EPS))
        o_ref[0, rows, :] = (jnp.concatenate(outs, axis=1) * nw_ref[...]).astype(BF16)

    state[...] = st
    for j in range(n_tiles):
        xpad[j, 0:8, :] = xpad[j, n_chunks * BLOCK:n_chunks * BLOCK + 8, :]


def _ssd(xbc, dt, z, cw, cb, dtb, alog, dsk, nw, e_mat, tm):
    bsz, lp, _ = xbc.shape
    blk = lambda width: pl.BlockSpec((1, tm, width), lambda b, c: (b, c, 0))
    return pl.pallas_call(
        _ssd_kernel,
        grid=(bsz, lp // tm),
        in_specs=[blk(XBC_W), blk(LANES), blk(SSM_INNER),
                  _const_spec((SSM_CONV, XBC_W)), _const_spec((1, XBC_W)), _const_spec((1, LANES)),
                  _const_spec((1, LANES)), _const_spec((1, SSM_INNER)), _const_spec((1, SSM_INNER)),
                  _const_spec((LANES, SSM_INNER))],
        out_specs=blk(SSM_INNER),
        out_shape=jax.ShapeDtypeStruct((bsz, lp, SSM_INNER), BF16),
        scratch_shapes=[pltpu.VMEM((BLOCK, SSM_INNER), F32), pltpu.VMEM((XBC_W // LANES, 8 + tm, LANES), F32)],
        compiler_params=pltpu.CompilerParams(dimension_semantics=("parallel", "arbitrary"),
                                             vmem_limit_bytes=VMEM_LIMIT),
        name="ssd",
    )(xbc, dt, z, cw, cb, dtb, alog, dsk, nw, e_mat)


def _mlp_tail(o_ref, g_ref, wu_ref, wd_ref):
    x = o_ref[...]
    u = _rms(x, g_ref[...]).astype(BF16)
    acc = None
    for c in range(D_FF // FF_CHUNK):
        a = jnp.dot(u, wu_ref[:, c * FF_CHUNK:(c + 1) * FF_CHUNK], preferred_element_type=F32)
        a = jnp.maximum(a, 0.0)
        a = (a * a).astype(BF16)
        d = jnp.dot(a, wd_ref[c * FF_CHUNK:(c + 1) * FF_CHUNK, :], preferred_element_type=F32)
        acc = d if acc is None else acc + d
    o_ref[...] = x + acc


def _post_even_kernel(load, n_src, *refs):
    att_ref, ssm_ref, wa_ref, ws_ref, g_ref, wu_ref, wd_ref, o_ref = refs[n_src:]
    o_ref[...] = (load(refs[:n_src]) + jnp.dot(att_ref[...], wa_ref[...], preferred_element_type=F32)
                  + jnp.dot(ssm_ref[...], ws_ref[...], preferred_element_type=F32))
    _mlp_tail(o_ref, g_ref, wu_ref, wd_ref)


def _post_even(src, att, ssm, wa, ws, g, wu, wd, tm, tiles_per_batch):
    rows = att.shape[0]
    row = lambda width: pl.BlockSpec((tm, width), lambda i: (i, 0))
    src_specs, src_args, load = _stream(src, tm, tiles_per_batch)
    return pl.pallas_call(
        functools.partial(_post_even_kernel, load, len(src_args)),
        grid=(rows // tm,),
        in_specs=src_specs + [row(Q_W), row(SSM_INNER), _const_spec((Q_W, D_MODEL)),
                              _const_spec((SSM_INNER, D_MODEL)), _const_spec((1, D_MODEL)),
                              _const_spec((D_MODEL, D_FF)), _const_spec((D_FF, D_MODEL))],
        out_specs=row(D_MODEL),
        out_shape=jax.ShapeDtypeStruct((rows, D_MODEL), F32),
        compiler_params=_params(1),
        name="post_even",
    )(*src_args, att, ssm, wa, ws, g, wu, wd)


def _glu_kernel(h_ref, g_ref, w_ref, b_ref, o_ref):
    u = _rms(h_ref[...], g_ref[...]).astype(BF16)
    p = jnp.dot(u, w_ref[...], preferred_element_type=F32) + b_ref[...]
    o_ref[...] = p[:, :D_MODEL] * _sigmoid(p[:, D_MODEL:])


def _glu(h, g, w, b, tm):
    rows = h.shape[0]
    row = pl.BlockSpec((tm, D_MODEL), lambda i: (i, 0))
    return pl.pallas_call(
        _glu_kernel,
        grid=(rows // tm,),
        in_specs=[row, _const_spec((1, D_MODEL)), _const_spec((D_MODEL, 2 * D_MODEL)),
                  _const_spec((1, 2 * D_MODEL))],
        out_specs=row,
        out_shape=jax.ShapeDtypeStruct((rows, D_MODEL), F32),
        compiler_params=_params(1),
        name="glu",
    )(h, g, w, b)


def _conv_fill(main, halo, buf):
    tm = main.shape[0]
    for j in range(D_MODEL // LANES):
        buf[j, 0:HALO, :] = halo[:, j * LANES:(j + 1) * LANES]
        buf[j, HALO:HALO + tm, :] = main[:, j * LANES:(j + 1) * LANES]


def _conv_unit(dw_ref, db_ref, buf, cv, r0, j):
    cols = slice(j * LANES, (j + 1) * LANES)
    acc = jnp.broadcast_to(db_ref[:, cols], (CONV_ROWS, LANES))
    for k in range(CONF_KERNEL):
        off = HALO - (CONF_KERNEL - 1) + k
        acc = acc + dw_ref[k:k + 1, cols] * buf[j, pl.ds(r0 + off, CONV_ROWS), :]
    cv[pl.ds(r0, CONV_ROWS), cols] = acc


def _post_odd_kernel(tiles_per_batch, final, h_ref, u_ref, halo_ref, dw_ref, db_ref, lg_ref, lb_ref, w2_ref,
                     b2_ref, g_ref, wu_ref, wd_ref, o_ref, buf, cv):
    tm = h_ref.shape[0]
    first = (pl.program_id(0) % tiles_per_batch) == 0
    if final:
        hrow = lax.broadcasted_iota(jnp.int32, (HALO, 1), 0)
        halo = jnp.where(first & (hrow < HALO - N_META), 0.0, halo_ref[...])
        main = u_ref[...]
    else:
        row = lax.broadcasted_iota(jnp.int32, (tm, 1), 0)
        halo = jnp.where(first, 0.0, halo_ref[...])
        main = jnp.where(first & (row < FRONT_PAD), 0.0, u_ref[...])
    _conv_fill(main, halo, buf)

    def body(r, carry):
        for j in range(D_MODEL // LANES):
            _conv_unit(dw_ref, db_ref, buf, cv, pl.multiple_of(r * CONV_ROWS, CONV_ROWS), j)
        return carry

    lax.fori_loop(0, tm // CONV_ROWS, body, 0)
    c = cv[...]
    mu = jnp.mean(c, axis=-1, keepdims=True)
    d = c - mu
    var = jnp.mean(d * d, axis=-1, keepdims=True)
    y = _silu(d * lax.rsqrt(var + LN_EPS) * lg_ref[...] + lb_ref[...])
    o_ref[...] = h_ref[...] + jnp.dot(y.astype(BF16), w2_ref[...], preferred_element_type=F32) + b2_ref[...]
    _mlp_tail(o_ref, g_ref, wu_ref, wd_ref)


def _post_odd(h, u, dw, db, lg, lb, w2, b2, g, wu, wd, tm, tiles_per_batch, final_lp=None):
    out_rows = h.shape[0] if final_lp is None else tiles_per_batch * tm * (h.shape[0] // final_lp)
    row = pl.BlockSpec((tm, D_MODEL), lambda i: (i, 0))
    if final_lp is None:
        per = tm // HALO
        src = row
        halo = pl.BlockSpec((HALO, D_MODEL), lambda i: (jnp.maximum(i * per - 1, 0), 0))
    else:
        def start(i):
            return (i // tiles_per_batch) * final_lp + BLOCK + (i % tiles_per_batch) * tm
        src = pl.BlockSpec((pl.Element(tm), pl.Element(D_MODEL)), lambda i: (pl.multiple_of(start(i), BLOCK), 0))
        halo = pl.BlockSpec((pl.Element(HALO), pl.Element(D_MODEL)),
                            lambda i: (pl.multiple_of(start(i) - HALO, HALO), 0))
    vec = _const_spec((1, D_MODEL))
    return pl.pallas_call(
        functools.partial(_post_odd_kernel, tiles_per_batch, final_lp is not None),
        grid=(out_rows // tm,),
        in_specs=[src, src, halo, _const_spec((CONF_KERNEL, D_MODEL)), vec, vec, vec,
                  _const_spec((D_MODEL, D_MODEL)), vec, vec,
                  _const_spec((D_MODEL, D_FF)), _const_spec((D_FF, D_MODEL))],
        out_specs=row,
        out_shape=jax.ShapeDtypeStruct((out_rows, D_MODEL), F32),
        scratch_shapes=[pltpu.VMEM((D_MODEL // LANES, HALO + tm, LANES), F32), pltpu.VMEM((tm, D_MODEL), F32)],
        compiler_params=_params(1),
        name="post_odd",
    )(h, u, u, dw, db, lg, lb, w2, b2, g, wu, wd)


def _row_tile(blocks_per_batch):
    for k in (5, 4, 3, 2, 1):
        if blocks_per_batch % k == 0:
            return k * BLOCK


def _pair_perm():
    cols = []
    for g in range(ATT_GROUP):
        for kv in range(ATT_KV_HEADS):
            head = kv * ATT_GROUP + g
            cols.extend(range(head * HEAD_DIM, (head + 1) * HEAD_DIM))
    return jnp.asarray(cols, dtype=jnp.int32)


def _expand_matrix():
    r = jnp.arange(LANES)[:, None]
    head = jnp.arange(SSM_INNER)[None, :] // SSM_HEAD_DIM
    return ((r < 3 * SSM_HEADS) & (r % SSM_HEADS == head)).astype(BF16)


def _pad_lanes(v, width=LANES):
    return jnp.pad(v, (0, width - v.shape[0]))[None, :]


def kernel(x, meta_tokens, mix_norm_even, w_in, ssm_conv_w, ssm_conv_b, dt_bias, a_log, d_skip, ssm_norm_w, q_norm, k_norm, sinks, w_out, mix_norm_odd, pw1_w, pw1_b, dw_w, dw_b, ln_g, ln_b, pw2_w, pw2_b, mlp_norm, w_up, w_down):
    bsz, seq, _ = x.shape
    assert seq % BLOCK == 0
    lp = seq + BLOCK
    nb = lp // BLOCK
    tm = _row_tile(nb)
    tpb = lp // tm
    rows = bsz * lp
    depth = mlp_norm.shape[0]
    assert depth % 2 == 0
    final_tm = next(k * BLOCK for k in (4, 3, 2, 1) if seq % (k * BLOCK) == 0)

    src = ("x", x.reshape(bsz * seq, D_MODEL), meta_tokens.astype(x.dtype))

    perm = _pair_perm()
    e_mat = _expand_matrix()
    s_q, s_k, s_v = Q_W, Q_W + KV_W, Q_W + 2 * KV_W
    s_z = s_v + SSM_INNER
    s_xbc = s_z + XBC_W

    for layer in range(depth):
        i = layer // 2
        wu = w_up[layer].astype(BF16)
        wd = w_down[layer].astype(BF16)
        mg = mlp_norm[layer][None, :]
        if layer % 2 == 0:
            wi = w_in[i]
            w_cat = jnp.concatenate(
                [wi[:, :s_q][:, perm], wi[:, s_q:s_xbc],
                 jnp.pad(wi[:, s_xbc:], ((0, 0), (0, LANES - SSM_HEADS)))], axis=1).astype(BF16)
            qg = jnp.tile(q_norm[i] * (HEAD_DIM ** -0.5), Q_W // HEAD_DIM)[None, :]
            kg = jnp.tile(k_norm[i], KV_W // HEAD_DIM)[None, :]
            q, kv, z, xbc, dt = _inproj(src, rows, mix_norm_even[i][None, :], w_cat, qg, kg, tm, tpb)
            att = _attention(sinks[i], q.reshape(bsz, lp, Q_W), kv.reshape(bsz, lp, 2 * LANES), tm)
            ssm = _ssd(xbc.reshape(bsz, lp, XBC_W), dt.reshape(bsz, lp, LANES), z.reshape(bsz, lp, SSM_INNER),
                       ssm_conv_w[i], ssm_conv_b[i][None, :], _pad_lanes(dt_bias[i]), _pad_lanes(a_log[i]),
                       jnp.repeat(d_skip[i], SSM_HEAD_DIM)[None, :], ssm_norm_w[i][None, :], e_mat, tm)
            wo = w_out[i]
            wa = wo[:Q_W][perm].astype(BF16)
            ws = wo[Q_W:].astype(BF16)
            h = _post_even(src, att.reshape(rows, Q_W), ssm.reshape(rows, SSM_INNER), wa, ws, mg, wu, wd, tm, tpb)
        else:
            u = _glu(h, mix_norm_odd[i][None, :], pw1_w[i].astype(BF16), pw1_b[i][None, :], tm)
            conv = (dw_w[i], dw_b[i][None, :], ln_g[i][None, :], ln_b[i][None, :],
                    pw2_w[i].astype(BF16), pw2_b[i][None, :], mg, wu, wd)
            if layer == depth - 1:
                h = _post_odd(h, u, *conv, final_tm, seq // final_tm, final_lp=lp)
            else:
                h = _post_odd(h, u, *conv, tm, tpb)
        src = ("h", h)
    return h.reshape(bsz, seq, D_MODEL)
```

```python
import functools

import jax
import jax.numpy as jnp
from jax import lax
from jax.experimental import pallas as pl
from jax.experimental.pallas import tpu as pltpu

D_MODEL = 1024
N_META = 16
BLOCK = 128
WINDOW = 128
ATT_HEADS = 8
ATT_KV_HEADS = 2
ATT_GROUP = ATT_HEADS // ATT_KV_HEADS
HEAD_DIM = 64
SSM_HEADS = 16
SSM_HEAD_DIM = 64
SSM_INNER = SSM_HEADS * SSM_HEAD_DIM
SSM_GROUPS = 2
SSM_STATE = 64
SSM_CONV = 4
CONF_KERNEL = 31
D_FF = 4 * D_MODEL
EPS = 1e-6
LN_EPS = 1e-5

Q_W = ATT_HEADS * HEAD_DIM
KV_W = ATT_KV_HEADS * HEAD_DIM
BC_W = SSM_GROUPS * SSM_STATE
XBC_W = SSM_INNER + 2 * BC_W
FRONT_PAD = BLOCK - N_META

LANES = 128
HALO = 32
CONV_ROWS = 128
FF_CHUNK = 1024
VMEM_LIMIT = 56 * 1024 * 1024
NEG = -1e30

BF16 = jnp.bfloat16
F32 = jnp.float32


def _const_spec(shape, layer=None):
    zeros = (0,) * len(shape)
    if layer is None:
        return pl.BlockSpec(shape, lambda *_: zeros, pipeline_mode=pl.Buffered(1))
    return pl.BlockSpec((None,) + tuple(shape), lambda *_: (layer,) + zeros, pipeline_mode=pl.Buffered(1))


def _params(n_axes, parallel=True):
    sem = ("parallel" if parallel else "arbitrary",) * n_axes
    return pltpu.CompilerParams(dimension_semantics=sem, vmem_limit_bytes=VMEM_LIMIT)


def _rms(x, g):
    return x * lax.rsqrt(jnp.mean(x * x, axis=-1, keepdims=True) + EPS) * g


def _sigmoid(x):
    return 0.5 + 0.5 * jnp.tanh(0.5 * x)


def _silu(x):
    h = 0.5 * x
    return h + h * jnp.tanh(h)


def _head_rms(x, gain):
    lane = lax.broadcasted_iota(jnp.int32, (1, LANES), 1)
    low = lane < HEAD_DIM
    outs = []
    for t in range(x.shape[1] // LANES):
        xt = x[:, t * LANES:(t + 1) * LANES]
        sq = xt * xt
        lo = jnp.sum(jnp.where(low, sq, 0.0), axis=-1, keepdims=True)
        hi = jnp.sum(jnp.where(low, 0.0, sq), axis=-1, keepdims=True)
        r = jnp.where(low, lax.rsqrt(lo * (1.0 / HEAD_DIM) + EPS), lax.rsqrt(hi * (1.0 / HEAD_DIM) + EPS))
        outs.append(xt * r * gain[:, t * LANES:(t + 1) * LANES])
    return outs[0] if len(outs) == 1 else jnp.concatenate(outs, axis=1)


def _split3(x):
    h1 = x.astype(BF16)
    r1 = x - h1.astype(F32)
    h2 = r1.astype(BF16)
    h3 = (r1 - h2.astype(F32)).astype(BF16)
    return h1, h2, h3


IN_COLS = Q_W + 2 * LANES + SSM_INNER + XBC_W + LANES


def _stream(src, tm, tiles_per_batch):
    if src[0] == "h":
        return [pl.BlockSpec((tm, D_MODEL), lambda i: (i, 0))], [src[1]], lambda refs: refs[0][...]
    _, x, meta = src
    assert tiles_per_batch >= 2
    seq = tiles_per_batch * tm - BLOCK

    def x_index(i):
        start = (i // tiles_per_batch) * seq + jnp.maximum((i % tiles_per_batch) * tm - BLOCK, 0)
        return pl.multiple_of(start, BLOCK), 0

    def load(refs):
        x_ref, meta_ref = refs
        xb = x_ref[...]
        front = jnp.concatenate([jnp.zeros((FRONT_PAD, D_MODEL), F32), meta_ref[...]], axis=0)
        first = jnp.concatenate([front, xb[:tm - BLOCK]], axis=0)
        return jnp.where(pl.program_id(0) % tiles_per_batch == 0, first, xb)

    specs = [pl.BlockSpec((pl.Element(tm), pl.Element(D_MODEL)), x_index), _const_spec((N_META, D_MODEL))]
    return specs, [x, meta], load


def _inproj_kernel(load, n_src, *refs):
    g_ref, w_ref, qg_ref, kg_ref, q_ref, kv_ref, z_ref, xbc_ref, dt_ref = refs[n_src:]
    u = _rms(load(refs[:n_src]), g_ref[...]).astype(BF16)
    p = jnp.dot(u, w_ref[...], preferred_element_type=F32)
    o = 0
    q = _head_rms(p[:, o:o + Q_W], qg_ref[...])
    o += Q_W
    k = _head_rms(p[:, o:o + LANES], kg_ref[...])
    o += LANES
    v = p[:, o:o + LANES]
    o += LANES
    q_ref[...] = q.astype(BF16)
    kv_ref[...] = jnp.concatenate([k, v], axis=1).astype(BF16)
    z_ref[...] = p[:, o:o + SSM_INNER]
    o += SSM_INNER
    xbc_ref[...] = p[:, o:o + XBC_W]
    o += XBC_W
    dt_ref[...] = p[:, o:o + LANES]


def _inproj(src, rows, g, w, li, qg, kg, tm, tiles_per_batch):
    row = lambda width: pl.BlockSpec((tm, width), lambda i: (i, 0))
    src_specs, src_args, load = _stream(src, tm, tiles_per_batch)
    return pl.pallas_call(
        functools.partial(_inproj_kernel, load, len(src_args)),
        grid=(rows // tm,),
        in_specs=src_specs + [_const_spec((1, D_MODEL)), _const_spec((D_MODEL, IN_COLS), li),
                              _const_spec((1, Q_W)), _const_spec((1, LANES))],
        out_specs=[row(Q_W), row(2 * LANES), row(SSM_INNER), row(XBC_W), row(LANES)],
        out_shape=[jax.ShapeDtypeStruct((rows, Q_W), BF16), jax.ShapeDtypeStruct((rows, 2 * LANES), BF16),
                   jax.ShapeDtypeStruct((rows, SSM_INNER), F32), jax.ShapeDtypeStruct((rows, XBC_W), F32),
                   jax.ShapeDtypeStruct((rows, LANES), F32)],
        compiler_params=_params(1),
        name="inproj",
    )(*src_args, g, w, qg, kg)


N_BIAS = 3


def _attn_kernel(sink_ref, q_ref, kv_ref, kvp_ref, kvm_ref, o_ref, bias_ref):
    t = pl.program_id(1)
    n_blocks = q_ref.shape[1] // BLOCK
    nq = ATT_GROUP * BLOCK
    nk = 2 * BLOCK + N_META
    col = lax.broadcasted_iota(jnp.int32, (1, nq), 1)
    grp = col // BLOCK

    @pl.when(t == 0)
    def _():
        for n in range(N_BIAS):
            q_pos = n * BLOCK + (col % BLOCK) - FRONT_PAD
            k_pos = (n - 1) * BLOCK + lax.broadcasted_iota(jnp.int32, (2 * BLOCK, 1), 0) - FRONT_PAD
            dist = q_pos - k_pos
            valid = (k_pos >= N_META) & (dist >= 0) & (dist < WINDOW)
            pen = jnp.abs(dist).astype(F32)
            dist_m = q_pos - lax.broadcasted_iota(jnp.int32, (N_META, 1), 0)
            valid_m = dist_m >= 0
            pen_m = jnp.minimum(jnp.abs(dist_m), WINDOW).astype(F32)
            for kv in range(ATT_KV_HEADS):
                slope = jnp.zeros((1, nq), F32)
                for g in range(ATT_GROUP):
                    slope = jnp.where(grp == g, 2.0 ** (-8.0 * (kv * ATT_GROUP + g + 1) / ATT_HEADS), slope)
                bias_ref[n, kv, 0:2 * BLOCK, :] = jnp.where(valid, -slope * pen, NEG)
                bias_ref[n, kv, 2 * BLOCK:nk, :] = jnp.where(valid_m, -slope * pen_m, NEG)

    lane = lax.broadcasted_iota(jnp.int32, (1, LANES), 1)
    sub = lax.broadcasted_iota(jnp.int32, (LANES, 1), 0)
    nt = (((1,), (1,)), ((), ()))
    halves = [lane < HEAD_DIM, lane >= HEAD_DIM]
    sinks = []
    for kv in range(ATT_KV_HEADS):
        sink = jnp.zeros((1, nq), F32)
        for g in range(ATT_GROUP):
            sink = jnp.where(grp == g, sink_ref[kv * ATT_GROUP + g], sink)
        sinks.append(sink)

    def transposed(v):
        return v.astype(F32).T.astype(BF16)

    kvm = kvm_ref[0, FRONT_PAD:BLOCK, :]
    k_meta = kvm[:, :LANES]
    vt_meta = transposed(kvm[:, LANES:])
    kv_prev = kvp_ref[0]
    k_prev = kv_prev[:, :LANES]
    vt_prev = transposed(kv_prev[:, LANES:])
    for b in range(n_blocks):
        rows = slice(b * BLOCK, (b + 1) * BLOCK)
        table = (N_BIAS - 1) if b >= N_BIAS - 1 else jnp.where(t == 0, b, N_BIAS - 1)
        q = q_ref[0, rows, :]
        qs = jnp.concatenate([q[:, g * LANES:(g + 1) * LANES] for g in range(ATT_GROUP)], axis=0)
        kv_cur = kv_ref[0, rows, :]
        k_cur = kv_cur[:, :LANES]
        vt_cur = transposed(kv_cur[:, LANES:])
        k_all = jnp.concatenate([k_prev, k_cur, k_meta], axis=0)
        v_t = jnp.concatenate([vt_prev, vt_cur, vt_meta], axis=1)
        zk = jnp.zeros_like(k_all)
        km = jnp.concatenate([jnp.where(halves[0], k_all, zk), jnp.where(halves[1], k_all, zk)], axis=0)
        s_both = lax.dot_general(km, qs, nt, preferred_element_type=F32)
        out_t = None
        for kv in range(ATT_KV_HEADS):
            s = s_both[kv * nk:(kv + 1) * nk] + bias_ref[table, kv]
            mx = jnp.maximum(jnp.max(s, axis=0, keepdims=True), sinks[kv])
            e = jnp.exp(s - mx)
            den = jnp.sum(e, axis=0, keepdims=True) + jnp.exp(sinks[kv] - mx)
            o_t = jnp.dot(v_t, e.astype(BF16), preferred_element_type=F32) * (1.0 / den)
            out_t = o_t if out_t is None else jnp.where(sub < HEAD_DIM, out_t, o_t)
        o_ref[0, rows, :] = jnp.concatenate([out_t[:, g * BLOCK:(g + 1) * BLOCK].T for g in range(ATT_GROUP)],
                                            axis=1).astype(BF16)
        k_prev, vt_prev = k_cur, vt_cur


def _attention(sinks, q, kv, tm):
    bsz, lp, _ = q.shape
    per = tm // BLOCK
    return pl.pallas_call(
        _attn_kernel,
        grid=(bsz, lp // tm),
        in_specs=[pl.BlockSpec(memory_space=pltpu.SMEM),
                  pl.BlockSpec((1, tm, Q_W), lambda b, t: (b, t, 0)),
                  pl.BlockSpec((1, tm, 2 * LANES), lambda b, t: (b, t, 0)),
                  pl.BlockSpec((1, BLOCK, 2 * LANES), lambda b, t: (b, jnp.maximum(t * per - 1, 0), 0)),
                  pl.BlockSpec((1, BLOCK, 2 * LANES), lambda b, t: (b, 0, 0))],
        out_specs=pl.BlockSpec((1, tm, Q_W), lambda b, t: (b, t, 0)),
        out_shape=jax.ShapeDtypeStruct((bsz, lp, Q_W), BF16),
        scratch_shapes=[pltpu.VMEM((N_BIAS, ATT_KV_HEADS, 2 * BLOCK + N_META, ATT_GROUP * BLOCK), F32)],
        compiler_params=pltpu.CompilerParams(dimension_semantics=("parallel", "arbitrary"),
                                             vmem_limit_bytes=VMEM_LIMIT),
        name="attention",
    )(sinks, q, kv, kv, kv)


def _ssd_kernel(xbc_ref, dt_ref, z_ref, cw_ref, cb_ref, dtb_ref, alog_ref, dsk_ref, nw_ref, e_ref,
                o_ref, state, xpad):
    c = pl.program_id(1)
    n_chunks = xbc_ref.shape[1] // BLOCK
    n_tiles = XBC_W // LANES

    for j in range(n_tiles):
        xpad[j, 8:, :] = xbc_ref[0, :, j * LANES:(j + 1) * LANES]

    @pl.when(c == 0)
    def _():
        state[...] = jnp.zeros_like(state)
        for j in range(n_tiles):
            xpad[j, 0:8 + FRONT_PAD, :] = jnp.zeros((8 + FRONT_PAD, LANES), F32)

    ri = lax.broadcasted_iota(jnp.int32, (BLOCK, BLOCK), 0)
    ci = lax.broadcasted_iota(jnp.int32, (BLOCK, BLOCK), 1)
    tril = ri >= ci
    tril_b = jnp.where(tril, 1.0, 0.0).astype(BF16)
    lane = lax.broadcasted_iota(jnp.int32, (1, LANES), 1)
    low = lane < SSM_STATE
    head_lane = lane < SSM_HEADS
    first_group = lax.broadcasted_iota(jnp.int32, (BLOCK, 1), 0) < SSM_STATE
    nt = (((1,), (1,)), ((), ()))
    hpg = SSM_HEADS // SSM_GROUPS
    gw = SSM_INNER // SSM_GROUPS
    a = -jnp.exp(alog_ref[...])
    e_mat = e_ref[...]
    st = state[...]

    for k in range(n_chunks):
        r0 = 8 + k * BLOCK
        rows = slice(k * BLOCK, (k + 1) * BLOCK)
        xa = []
        for j in range(n_tiles):
            cols = slice(j * LANES, (j + 1) * LANES)
            acc = cb_ref[:, cols]
            for s in range(SSM_CONV):
                acc = acc + cw_ref[SSM_CONV - 1 - s:SSM_CONV - s, cols] * xpad[j, r0 - s:r0 - s + BLOCK, :]
            xa.append(_silu(acc))
        xs = jnp.concatenate(xa[:SSM_INNER // LANES], axis=1)
        bm = xa[SSM_INNER // LANES]
        cm = xa[SSM_INNER // LANES + 1]

        dtr_t = (dt_ref[0, rows, :] + dtb_ref[...]).T[0:SSM_HEADS, :]
        dt_t = jnp.maximum(dtr_t, 0.0) + jnp.log1p(jnp.exp(-jnp.abs(dtr_t)))
        if k == 0:
            dt_t = jnp.where((c > 0) | (lane >= FRONT_PAD), dt_t, 0.0)
        dt = jnp.concatenate([dt_t, jnp.zeros((BLOCK - SSM_HEADS, BLOCK), F32)], axis=0).T
        adt = dt * a
        a_cs = sum(jnp.dot(tril_b, t, preferred_element_type=F32) for t in _split3(adt))
        a_last = a_cs[BLOCK - 1:BLOCK, :]
        wx = jnp.exp(a_last - a_cs) * dt
        ea = jnp.where(head_lane, jnp.exp(a_cs), 0.0)
        h1, h2, h3 = _split3(jnp.concatenate([wx, ea], axis=0))
        packed = (h1.astype(F32) + pltpu.roll(h2.astype(F32), SSM_HEADS, axis=1)
                  + pltpu.roll(h3.astype(F32), 2 * SSM_HEADS, axis=1)).astype(BF16)
        both_e = jnp.dot(packed, e_mat, preferred_element_type=F32)
        wx_e = both_e[:BLOCK]
        ea_e = both_e[BLOCK:]
        decay_e = ea_e[BLOCK - 1:BLOCK, :]
        a_cs_t = a_cs.T

        cm_b = cm.astype(BF16)
        bm_b = bm.astype(BF16)
        zero_b = jnp.zeros_like(cm_b)
        cm_split = jnp.concatenate([jnp.where(low, cm_b, zero_b), jnp.where(low, zero_b, cm_b)], axis=0)
        cb_both = lax.dot_general(cm_split, bm_b, nt, preferred_element_type=F32)
        cb_b = [cb_both[:BLOCK].astype(BF16), cb_both[BLOCK:].astype(BF16)]
        dt_tb = dt_t.astype(BF16)

        y_tiles = []
        for j in range(SSM_HEADS // 2):
            xp = xa[j].astype(BF16)
            zx = jnp.zeros_like(xp)
            rhs = jnp.concatenate([jnp.where(low, xp, zx), jnp.where(low, zx, xp)], axis=0)
            ws = []
            for hd in (2 * j, 2 * j + 1):
                seg = a_cs[:, hd:hd + 1] - a_cs_t[hd:hd + 1, :]
                decay = jnp.exp(jnp.where(tril, seg, NEG))
                ws.append(cb_b[hd // hpg] * decay.astype(BF16) * dt_tb[hd:hd + 1, :])
            y_tiles.append(jnp.dot(jnp.concatenate(ws, axis=1), rhs, preferred_element_type=F32))
        y = jnp.concatenate(y_tiles, axis=1)

        y = y + jnp.dot(cm_b, st.astype(BF16), preferred_element_type=F32) * ea_e
        bm_t = bm.T.astype(BF16)
        zero_t = jnp.zeros_like(bm_t)
        xw = jnp.concatenate([t.astype(BF16) for t in xa[:SSM_INNER // LANES]], axis=1) * wx_e.astype(BF16)
        upd = jnp.concatenate(
            [jnp.dot(jnp.where(first_group, bm_t, zero_t), xw[:, :gw], preferred_element_type=F32),
             jnp.dot(jnp.where(first_group, zero_t, bm_t), xw[:, gw:], preferred_element_type=F32)], axis=1)
        st = st * decay_e + upd

        y = y + xs * dsk_ref[...]
        y = y * _silu(z_ref[0, rows, :])
        outs = []
        for g in range(SSM_GROUPS):
            yg = y[:, g * gw:(g + 1) * gw]
            outs.append(yg * lax.rsqrt(jnp.mean(yg * yg, axis=-1, keepdims=True) + EPS))
        o_ref[0, rows, :] = (jnp.concatenate(outs, axis=1) * nw_ref[...]).astype(BF16)

    state[...] = st
    for j in range(n_tiles):
        xpad[j, 0:8, :] = xpad[j, n_chunks * BLOCK:n_chunks * BLOCK + 8, :]


def _ssd(xbc, dt, z, cw, cb, dtb, alog, dsk, nw, e_mat, tm):
    bsz, lp, _ = xbc.shape
    blk = lambda width: pl.BlockSpec((1, tm, width), lambda b, c: (b, c, 0))
    return pl.pallas_call(
        _ssd_kernel,
        grid=(bsz, lp // tm),
        in_specs=[blk(XBC_W), blk(LANES), blk(SSM_INNER),
                  _const_spec((SSM_CONV, XBC_W)), _const_spec((1, XBC_W)), _const_spec((1, LANES)),
                  _const_spec((1, LANES)), _const_spec((1, SSM_INNER)), _const_spec((1, SSM_INNER)),
                  _const_spec((LANES, SSM_INNER))],
        out_specs=blk(SSM_INNER),
        out_shape=jax.ShapeDtypeStruct((bsz, lp, SSM_INNER), BF16),
        scratch_shapes=[pltpu.VMEM((BLOCK, SSM_INNER), F32), pltpu.VMEM((XBC_W // LANES, 8 + tm, LANES), F32)],
        compiler_params=pltpu.CompilerParams(dimension_semantics=("parallel", "arbitrary"),
                                             vmem_limit_bytes=VMEM_LIMIT),
        name="ssd",
    )(xbc, dt, z, cw, cb, dtb, alog, dsk, nw, e_mat)


def _mlp_tail(o_ref, g_ref, wu_ref, wd_ref):
    x = o_ref[...]
    u = _rms(x, g_ref[...]).astype(BF16)
    acc = None
    for c in range(D_FF // FF_CHUNK):
        a = jnp.dot(u, wu_ref[:, c * FF_CHUNK:(c + 1) * FF_CHUNK], preferred_element_type=F32)
        a = jnp.maximum(a, 0.0)
        a = (a * a).astype(BF16)
        d = jnp.dot(a, wd_ref[c * FF_CHUNK:(c + 1) * FF_CHUNK, :], preferred_element_type=F32)
        acc = d if acc is None else acc + d
    o_ref[...] = x + acc


def _post_even_kernel(load, n_src, *refs):
    att_ref, ssm_ref, wa_ref, ws_ref, g_ref, wu_ref, wd_ref, o_ref = refs[n_src:]
    o_ref[...] = (load(refs[:n_src]) + jnp.dot(att_ref[...], wa_ref[...], preferred_element_type=F32)
                  + jnp.dot(ssm_ref[...], ws_ref[...], preferred_element_type=F32))
    _mlp_tail(o_ref, g_ref, wu_ref, wd_ref)


def _post_even(src, att, ssm, wa, ws, li, g, wu, wd, layer, tm, tiles_per_batch):
    rows = att.shape[0]
    row = lambda width: pl.BlockSpec((tm, width), lambda i: (i, 0))
    src_specs, src_args, load = _stream(src, tm, tiles_per_batch)
    return pl.pallas_call(
        functools.partial(_post_even_kernel, load, len(src_args)),
        grid=(rows // tm,),
        in_specs=src_specs + [row(Q_W), row(SSM_INNER), _const_spec((Q_W, D_MODEL), li),
                              _const_spec((SSM_INNER, D_MODEL), li), _const_spec((1, D_MODEL)),
                              _const_spec((D_MODEL, D_FF), layer), _const_spec((D_FF, D_MODEL), layer)],
        out_specs=row(D_MODEL),
        out_shape=jax.ShapeDtypeStruct((rows, D_MODEL), F32),
        compiler_params=_params(1),
        name="post_even",
    )(*src_args, att, ssm, wa, ws, g, wu, wd)


def _glu_kernel(h_ref, g_ref, w_ref, b_ref, o_ref):
    u = _rms(h_ref[...], g_ref[...]).astype(BF16)
    p = jnp.dot(u, w_ref[...], preferred_element_type=F32) + b_ref[...]
    o_ref[...] = p[:, :D_MODEL] * _sigmoid(p[:, D_MODEL:])


def _glu(h, g, w, li, b, tm):
    rows = h.shape[0]
    row = pl.BlockSpec((tm, D_MODEL), lambda i: (i, 0))
    return pl.pallas_call(
        _glu_kernel,
        grid=(rows // tm,),
        in_specs=[row, _const_spec((1, D_MODEL)), _const_spec((D_MODEL, 2 * D_MODEL), li),
                  _const_spec((1, 2 * D_MODEL))],
        out_specs=row,
        out_shape=jax.ShapeDtypeStruct((rows, D_MODEL), F32),
        compiler_params=_params(1),
        name="glu",
    )(h, g, w, b)


def _conv_fill(main, halo, buf):
    tm = main.shape[0]
    for j in range(D_MODEL // LANES):
        buf[j, 0:HALO, :] = halo[:, j * LANES:(j + 1) * LANES]
        buf[j, HALO:HALO + tm, :] = main[:, j * LANES:(j + 1) * LANES]


def _conv_unit(dw_ref, db_ref, buf, cv, r0, j):
    cols = slice(j * LANES, (j + 1) * LANES)
    acc = jnp.broadcast_to(db_ref[:, cols], (CONV_ROWS, LANES))
    for k in range(CONF_KERNEL):
        off = HALO - (CONF_KERNEL - 1) + k
        acc = acc + dw_ref[k:k + 1, cols] * buf[j, pl.ds(r0 + off, CONV_ROWS), :]
    cv[pl.ds(r0, CONV_ROWS), cols] = acc


def _post_odd_kernel(tiles_per_batch, final, h_ref, u_ref, halo_ref, dw_ref, db_ref, lg_ref, lb_ref, w2_ref,
                     b2_ref, g_ref, wu_ref, wd_ref, o_ref, buf, cv):
    tm = h_ref.shape[0]
    first = (pl.program_id(0) % tiles_per_batch) == 0
    if final:
        hrow = lax.broadcasted_iota(jnp.int32, (HALO, 1), 0)
        halo = jnp.where(first & (hrow < HALO - N_META), 0.0, halo_ref[...])
        main = u_ref[...]
    else:
        row = lax.broadcasted_iota(jnp.int32, (tm, 1), 0)
        halo = jnp.where(first, 0.0, halo_ref[...])
        main = jnp.where(first & (row < FRONT_PAD), 0.0, u_ref[...])
    _conv_fill(main, halo, buf)

    def body(r, carry):
        for j in range(D_MODEL // LANES):
            _conv_unit(dw_ref, db_ref, buf, cv, pl.multiple_of(r * CONV_ROWS, CONV_ROWS), j)
        return carry

    lax.fori_loop(0, tm // CONV_ROWS, body, 0)
    c = cv[...]
    mu = jnp.mean(c, axis=-1, keepdims=True)
    d = c - mu
    var = jnp.mean(d * d, axis=-1, keepdims=True)
    y = _silu(d * lax.rsqrt(var + LN_EPS) * lg_ref[...] + lb_ref[...])
    o_ref[...] = h_ref[...] + jnp.dot(y.astype(BF16), w2_ref[...], preferred_element_type=F32) + b2_ref[...]
    _mlp_tail(o_ref, g_ref, wu_ref, wd_ref)


def _post_odd(h, u, dw, db, lg, lb, w2, li, b2, g, wu, wd, layer, tm, tiles_per_batch, final_lp=None):
    out_rows = h.shape[0] if final_lp is None else tiles_per_batch * tm * (h.shape[0] // final_lp)
    row = pl.BlockSpec((tm, D_MODEL), lambda i: (i, 0))
    if final_lp is None:
        per = tm // HALO
        src = row
        halo = pl.BlockSpec((HALO, D_MODEL), lambda i: (jnp.maximum(i * per - 1, 0), 0))
    else:
        def start(i):
            return (i // tiles_per_batch) * final_lp + BLOCK + (i % tiles_per_batch) * tm
        src = pl.BlockSpec((pl.Element(tm), pl.Element(D_MODEL)), lambda i: (pl.multiple_of(start(i), BLOCK), 0))
        halo = pl.BlockSpec((pl.Element(HALO), pl.Element(D_MODEL)),
                            lambda i: (pl.multiple_of(start(i) - HALO, HALO), 0))
    vec = _const_spec((1, D_MODEL))
    return pl.pallas_call(
        functools.partial(_post_odd_kernel, tiles_per_batch, final_lp is not None),
        grid=(out_rows // tm,),
        in_specs=[src, src, halo, _const_spec((CONF_KERNEL, D_MODEL)), vec, vec, vec,
                  _const_spec((D_MODEL, D_MODEL), li), vec, vec,
                  _const_spec((D_MODEL, D_FF), layer), _const_spec((D_FF, D_MODEL), layer)],
        out_specs=row,
        out_shape=jax.ShapeDtypeStruct((out_rows, D_MODEL), F32),
        scratch_shapes=[pltpu.VMEM((D_MODEL // LANES, HALO + tm, LANES), F32), pltpu.VMEM((tm, D_MODEL), F32)],
        compiler_params=_params(1),
        name="post_odd",
    )(h, u, u, dw, db, lg, lb, w2, b2, g, wu, wd)


def _row_tile(blocks_per_batch, most=5):
    return next(k * BLOCK for k in range(most, 0, -1) if blocks_per_batch % k == 0)


def _pair_perm():
    cols = []
    for g in range(ATT_GROUP):
        for kv in range(ATT_KV_HEADS):
            head = kv * ATT_GROUP + g
            cols.extend(range(head * HEAD_DIM, (head + 1) * HEAD_DIM))
    return jnp.asarray(cols, dtype=jnp.int32)


def _expand_matrix():
    r = jnp.arange(LANES)[:, None]
    head = jnp.arange(SSM_INNER)[None, :] // SSM_HEAD_DIM
    return ((r < 3 * SSM_HEADS) & (r % SSM_HEADS == head)).astype(BF16)


def _pad_lanes(v, width=LANES):
    return jnp.pad(v, (0, width - v.shape[0]))[None, :]


def kernel(x, meta_tokens, mix_norm_even, w_in, ssm_conv_w, ssm_conv_b, dt_bias, a_log, d_skip, ssm_norm_w, q_norm, k_norm, sinks, w_out, mix_norm_odd, pw1_w, pw1_b, dw_w, dw_b, ln_g, ln_b, pw2_w, pw2_b, mlp_norm, w_up, w_down):
    bsz, seq, _ = x.shape
    assert seq % BLOCK == 0
    lp = seq + BLOCK
    nb = lp // BLOCK
    tm = _row_tile(nb)
    att_tm = _row_tile(nb, most=13)
    tpb = lp // tm
    rows = bsz * lp
    depth = mlp_norm.shape[0]
    assert depth % 2 == 0
    final_tm = next(k * BLOCK for k in (4, 3, 2, 1) if seq % (k * BLOCK) == 0)

    src = ("x", x.reshape(bsz * seq, D_MODEL), meta_tokens.astype(x.dtype))

    perm = _pair_perm()
    e_mat = _expand_matrix()
    s_q, s_xbc = Q_W, Q_W + 2 * KV_W + SSM_INNER + XBC_W
    w_in_b = w_in.astype(BF16)
    w_cat = jnp.concatenate([w_in_b[:, :, :s_q][:, :, perm], w_in_b[:, :, s_q:s_xbc],
                             jnp.pad(w_in_b[:, :, s_xbc:], ((0, 0), (0, 0), (0, LANES - SSM_HEADS)))], axis=2)
    w_out_b = w_out.astype(BF16)
    wa = w_out_b[:, :Q_W][:, perm]
    ws = w_out_b[:, Q_W:]
    wu = w_up.astype(BF16)
    wd = w_down.astype(BF16)
    w1 = pw1_w.astype(BF16)
    w2 = pw2_w.astype(BF16)
    glu_tm = 2 * tm if rows % (2 * tm) == 0 else tm

    for layer in range(depth):
        i = layer // 2
        mg = mlp_norm[layer][None, :]
        if layer % 2 == 0:
            qg = jnp.tile(q_norm[i] * (HEAD_DIM ** -0.5), Q_W // HEAD_DIM)[None, :]
            kg = jnp.tile(k_norm[i], KV_W // HEAD_DIM)[None, :]
            q, kv, z, xbc, dt = _inproj(src, rows, mix_norm_even[i][None, :], w_cat, i, qg, kg, tm, tpb)
            att = _attention(sinks[i], q.reshape(bsz, lp, Q_W), kv.reshape(bsz, lp, 2 * LANES), att_tm)
            ssm = _ssd(xbc.reshape(bsz, lp, XBC_W), dt.reshape(bsz, lp, LANES), z.reshape(bsz, lp, SSM_INNER),
                       ssm_conv_w[i], ssm_conv_b[i][None, :], _pad_lanes(dt_bias[i]), _pad_lanes(a_log[i]),
                       jnp.repeat(d_skip[i], SSM_HEAD_DIM)[None, :], ssm_norm_w[i][None, :], e_mat, tm)
            h = _post_even(src, att.reshape(rows, Q_W), ssm.reshape(rows, SSM_INNER), wa, ws, i, mg, wu, wd, layer,
                           tm, tpb)
        else:
            u = _glu(h, mix_norm_odd[i][None, :], w1, i, pw1_b[i][None, :], glu_tm)
            conv = (dw_w[i], dw_b[i][None, :], ln_g[i][None, :], ln_b[i][None, :], w2, i, pw2_b[i][None, :],
                    mg, wu, wd, layer)
            if layer == depth - 1:
                h = _post_odd(h, u, *conv, final_tm, seq // final_tm, final_lp=lp)
            else:
                h = _post_odd(h, u, *conv, tm, tpb)
        src = ("h", h)
    return h.reshape(bsz, seq, D_MODEL)
```

```python
import functools

import jax
import jax.numpy as jnp
from jax import lax
from jax.experimental import pallas as pl
from jax.experimental.pallas import tpu as pltpu

D_MODEL = 1024
N_META = 16
BLOCK = 128
WINDOW = 128
ATT_HEADS = 8
ATT_KV_HEADS = 2
ATT_GROUP = ATT_HEADS // ATT_KV_HEADS
HEAD_DIM = 64
SSM_HEADS = 16
SSM_HEAD_DIM = 64
SSM_INNER = SSM_HEADS * SSM_HEAD_DIM
SSM_GROUPS = 2
SSM_STATE = 64
SSM_CONV = 4
CONF_KERNEL = 31
D_FF = 4 * D_MODEL
EPS = 1e-6
LN_EPS = 1e-5

Q_W = ATT_HEADS * HEAD_DIM
KV_W = ATT_KV_HEADS * HEAD_DIM
BC_W = SSM_GROUPS * SSM_STATE
XBC_W = SSM_INNER + 2 * BC_W
FRONT_PAD = BLOCK - N_META

LANES = 128
HALO = 32
CONV_ROWS = 128
FF_CHUNK = 1024
VMEM_LIMIT = 56 * 1024 * 1024
NEG = -1e30

BF16 = jnp.bfloat16
F32 = jnp.float32


def _const_spec(shape, layer=None):
    zeros = (0,) * len(shape)
    if layer is None:
        return pl.BlockSpec(shape, lambda *_: zeros, pipeline_mode=pl.Buffered(1))
    return pl.BlockSpec((None,) + tuple(shape), lambda *_: (layer,) + zeros, pipeline_mode=pl.Buffered(1))


def _params(*semantics):
    return pltpu.CompilerParams(dimension_semantics=semantics, vmem_limit_bytes=VMEM_LIMIT)


def _rms(x, g):
    return x * lax.rsqrt(jnp.mean(x * x, axis=-1, keepdims=True) + EPS) * g


def _sigmoid(x):
    return 0.5 + 0.5 * jnp.tanh(0.5 * x)


def _silu(x):
    h = 0.5 * x
    return h + h * jnp.tanh(h)


def _head_rms(x, gain):
    lane = lax.broadcasted_iota(jnp.int32, (1, LANES), 1)
    low = lane < HEAD_DIM
    outs = []
    for t in range(x.shape[1] // LANES):
        xt = x[:, t * LANES:(t + 1) * LANES]
        sq = xt * xt
        lo = jnp.sum(jnp.where(low, sq, 0.0), axis=-1, keepdims=True)
        hi = jnp.sum(jnp.where(low, 0.0, sq), axis=-1, keepdims=True)
        r = jnp.where(low, lax.rsqrt(lo * (1.0 / HEAD_DIM) + EPS), lax.rsqrt(hi * (1.0 / HEAD_DIM) + EPS))
        outs.append(xt * r * gain[:, t * LANES:(t + 1) * LANES])
    return outs[0] if len(outs) == 1 else jnp.concatenate(outs, axis=1)


def _split3(x):
    h1 = x.astype(BF16)
    r1 = x - h1.astype(F32)
    h2 = r1.astype(BF16)
    h3 = (r1 - h2.astype(F32)).astype(BF16)
    return h1, h2, h3


IN_COLS = Q_W + 2 * LANES + SSM_INNER + XBC_W + LANES


def _stream(src, tm, tiles_per_batch):
    if src[0] == "h":
        return [pl.BlockSpec((tm, D_MODEL), lambda i: (i, 0))], [src[1]], lambda refs: refs[0][...]
    _, x, meta = src
    assert tiles_per_batch >= 2
    seq = tiles_per_batch * tm - BLOCK

    def x_index(i):
        start = (i // tiles_per_batch) * seq + jnp.maximum((i % tiles_per_batch) * tm - BLOCK, 0)
        return pl.multiple_of(start, BLOCK), 0

    def load(refs):
        x_ref, meta_ref = refs
        xb = x_ref[...]
        front = jnp.concatenate([jnp.zeros((FRONT_PAD, D_MODEL), F32), meta_ref[...]], axis=0)
        first = jnp.concatenate([front, xb[:tm - BLOCK]], axis=0)
        return jnp.where(pl.program_id(0) % tiles_per_batch == 0, first, xb)

    specs = [pl.BlockSpec((pl.Element(tm), pl.Element(D_MODEL)), x_index), _const_spec((N_META, D_MODEL))]
    return specs, [x, meta], load


def _inproj_kernel(load, n_src, *refs):
    g_ref, w_ref, qg_ref, kg_ref, q_ref, kv_ref, z_ref, xbc_ref, dt_ref = refs[n_src:]
    u = _rms(load(refs[:n_src]), g_ref[...]).astype(BF16)
    p = jnp.dot(u, w_ref[...], preferred_element_type=F32)
    o = 0
    q = _head_rms(p[:, o:o + Q_W], qg_ref[...])
    o += Q_W
    k = _head_rms(p[:, o:o + LANES], kg_ref[...])
    o += LANES
    v = p[:, o:o + LANES]
    o += LANES
    q_ref[...] = q.astype(BF16)
    kv_ref[...] = jnp.concatenate([k, v], axis=1).astype(BF16)
    z_ref[...] = p[:, o:o + SSM_INNER]
    o += SSM_INNER
    xbc_ref[...] = p[:, o:o + XBC_W]
    o += XBC_W
    dt_ref[...] = p[:, o:o + LANES]


def _inproj(src, rows, g, w, li, qg, kg, tm, tiles_per_batch):
    row = lambda width: pl.BlockSpec((tm, width), lambda i: (i, 0))
    src_specs, src_args, load = _stream(src, tm, tiles_per_batch)
    return pl.pallas_call(
        functools.partial(_inproj_kernel, load, len(src_args)),
        grid=(rows // tm,),
        in_specs=src_specs + [_const_spec((1, D_MODEL)), _const_spec((D_MODEL, IN_COLS), li),
                              _const_spec((1, Q_W)), _const_spec((1, LANES))],
        out_specs=[row(Q_W), row(2 * LANES), row(SSM_INNER), row(XBC_W), row(LANES)],
        out_shape=[jax.ShapeDtypeStruct((rows, Q_W), BF16), jax.ShapeDtypeStruct((rows, 2 * LANES), BF16),
                   jax.ShapeDtypeStruct((rows, SSM_INNER), F32), jax.ShapeDtypeStruct((rows, XBC_W), F32),
                   jax.ShapeDtypeStruct((rows, LANES), F32)],
        compiler_params=_params("parallel"),
        name="inproj",
    )(*src_args, g, w, qg, kg)


N_BIAS = 3


def _attn_kernel(sink_ref, q_ref, kv_ref, kvp_ref, kvm_ref, o_ref, bias_ref):
    t = pl.program_id(1)
    n_blocks = q_ref.shape[1] // BLOCK
    nq = ATT_GROUP * BLOCK
    nk = 2 * BLOCK + N_META
    col = lax.broadcasted_iota(jnp.int32, (1, nq), 1)
    grp = col // BLOCK

    @pl.when(t == 0)
    def _():
        for n in range(N_BIAS):
            q_pos = n * BLOCK + (col % BLOCK) - FRONT_PAD
            k_pos = (n - 1) * BLOCK + lax.broadcasted_iota(jnp.int32, (2 * BLOCK, 1), 0) - FRONT_PAD
            dist = q_pos - k_pos
            valid = (k_pos >= N_META) & (dist >= 0) & (dist < WINDOW)
            pen = jnp.abs(dist).astype(F32)
            dist_m = q_pos - lax.broadcasted_iota(jnp.int32, (N_META, 1), 0)
            valid_m = dist_m >= 0
            pen_m = jnp.minimum(jnp.abs(dist_m), WINDOW).astype(F32)
            for kv in range(ATT_KV_HEADS):
                slope = jnp.zeros((1, nq), F32)
                for g in range(ATT_GROUP):
                    slope = jnp.where(grp == g, 2.0 ** (-8.0 * (kv * ATT_GROUP + g + 1) / ATT_HEADS), slope)
                bias_ref[n, kv, 0:2 * BLOCK, :] = jnp.where(valid, -slope * pen, NEG)
                bias_ref[n, kv, 2 * BLOCK:nk, :] = jnp.where(valid_m, -slope * pen_m, NEG)

    lane = lax.broadcasted_iota(jnp.int32, (1, LANES), 1)
    sub = lax.broadcasted_iota(jnp.int32, (LANES, 1), 0)
    nt = (((1,), (1,)), ((), ()))
    halves = [lane < HEAD_DIM, lane >= HEAD_DIM]
    sinks = []
    for kv in range(ATT_KV_HEADS):
        sink = jnp.zeros((1, nq), F32)
        for g in range(ATT_GROUP):
            sink = jnp.where(grp == g, sink_ref[kv * ATT_GROUP + g], sink)
        sinks.append(sink)

    def transposed(v):
        return v.astype(F32).T.astype(BF16)

    kvm = kvm_ref[0, FRONT_PAD:BLOCK, :]
    k_meta = kvm[:, :LANES]
    vt_meta = transposed(kvm[:, LANES:])
    kv_prev = kvp_ref[0]
    k_prev = kv_prev[:, :LANES]
    vt_prev = transposed(kv_prev[:, LANES:])
    for b in range(n_blocks):
        rows = slice(b * BLOCK, (b + 1) * BLOCK)
        table = (N_BIAS - 1) if b >= N_BIAS - 1 else jnp.where(t == 0, b, N_BIAS - 1)
        q = q_ref[0, rows, :]
        qs = jnp.concatenate([q[:, g * LANES:(g + 1) * LANES] for g in range(ATT_GROUP)], axis=0)
        kv_cur = kv_ref[0, rows, :]
        k_cur = kv_cur[:, :LANES]
        vt_cur = transposed(kv_cur[:, LANES:])
        k_all = jnp.concatenate([k_prev, k_cur, k_meta], axis=0)
        v_t = jnp.concatenate([vt_prev, vt_cur, vt_meta], axis=1)
        zk = jnp.zeros_like(k_all)
        km = jnp.concatenate([jnp.where(halves[0], k_all, zk), jnp.where(halves[1], k_all, zk)], axis=0)
        s_both = lax.dot_general(km, qs, nt, preferred_element_type=F32)
        out_t = None
        for kv in range(ATT_KV_HEADS):
            s = s_both[kv * nk:(kv + 1) * nk] + bias_ref[table, kv]
            mx = jnp.maximum(jnp.max(s, axis=0, keepdims=True), sinks[kv])
            e = jnp.exp(s - mx)
            den = jnp.sum(e, axis=0, keepdims=True) + jnp.exp(sinks[kv] - mx)
            o_t = jnp.dot(v_t, e.astype(BF16), preferred_element_type=F32) * (1.0 / den)
            out_t = o_t if out_t is None else jnp.where(sub < HEAD_DIM, out_t, o_t)
        o_ref[0, rows, :] = jnp.concatenate([out_t[:, g * BLOCK:(g + 1) * BLOCK].T for g in range(ATT_GROUP)],
                                            axis=1).astype(BF16)
        k_prev, vt_prev = k_cur, vt_cur


def _attention(sinks, q, kv, tm):
    bsz, lp, _ = q.shape
    per = tm // BLOCK
    return pl.pallas_call(
        _attn_kernel,
        grid=(bsz, lp // tm),
        in_specs=[pl.BlockSpec(memory_space=pltpu.SMEM),
                  pl.BlockSpec((1, tm, Q_W), lambda b, t: (b, t, 0)),
                  pl.BlockSpec((1, tm, 2 * LANES), lambda b, t: (b, t, 0)),
                  pl.BlockSpec((1, BLOCK, 2 * LANES), lambda b, t: (b, jnp.maximum(t * per - 1, 0), 0)),
                  pl.BlockSpec((1, BLOCK, 2 * LANES), lambda b, t: (b, 0, 0))],
        out_specs=pl.BlockSpec((1, tm, Q_W), lambda b, t: (b, t, 0)),
        out_shape=jax.ShapeDtypeStruct((bsz, lp, Q_W), BF16),
        scratch_shapes=[pltpu.VMEM((N_BIAS, ATT_KV_HEADS, 2 * BLOCK + N_META, ATT_GROUP * BLOCK), F32)],
        compiler_params=_params("parallel", "arbitrary"),
        name="attention",
    )(sinks, q, kv, kv, kv)


def _ssd_kernel(xbc_ref, dt_ref, z_ref, cw_ref, cb_ref, dtb_ref, alog_ref, dsk_ref, nw_ref, e_ref,
                o_ref, state, xpad):
    c = pl.program_id(1)
    n_chunks = xbc_ref.shape[1] // BLOCK
    n_tiles = XBC_W // LANES

    for j in range(n_tiles):
        xpad[j, 8:, :] = xbc_ref[0, :, j * LANES:(j + 1) * LANES]

    @pl.when(c == 0)
    def _():
        state[...] = jnp.zeros_like(state)
        for j in range(n_tiles):
            xpad[j, 0:8 + FRONT_PAD, :] = jnp.zeros((8 + FRONT_PAD, LANES), F32)

    ri = lax.broadcasted_iota(jnp.int32, (BLOCK, BLOCK), 0)
    ci = lax.broadcasted_iota(jnp.int32, (BLOCK, BLOCK), 1)
    tril = ri >= ci
    tril_b = jnp.where(tril, 1.0, 0.0).astype(BF16)
    lane = lax.broadcasted_iota(jnp.int32, (1, LANES), 1)
    low = lane < SSM_STATE
    head_lane = lane < SSM_HEADS
    first_group = lax.broadcasted_iota(jnp.int32, (BLOCK, 1), 0) < SSM_STATE
    nt = (((1,), (1,)), ((), ()))
    hpg = SSM_HEADS // SSM_GROUPS
    gw = SSM_INNER // SSM_GROUPS
    a = -jnp.exp(alog_ref[...])
    e_mat = e_ref[...]
    st = state[...]

    for k in range(n_chunks):
        r0 = 8 + k * BLOCK
        rows = slice(k * BLOCK, (k + 1) * BLOCK)
        xa = []
        for j in range(n_tiles):
            cols = slice(j * LANES, (j + 1) * LANES)
            acc = cb_ref[:, cols]
            for s in range(SSM_CONV):
                acc = acc + cw_ref[SSM_CONV - 1 - s:SSM_CONV - s, cols] * xpad[j, r0 - s:r0 - s + BLOCK, :]
            xa.append(_silu(acc))
        xs = jnp.concatenate(xa[:SSM_INNER // LANES], axis=1)
        bm = xa[SSM_INNER // LANES]
        cm = xa[SSM_INNER // LANES + 1]

        dtr_t = (dt_ref[0, rows, :] + dtb_ref[...]).T[0:SSM_HEADS, :]
        dt_t = jnp.maximum(dtr_t, 0.0) + jnp.log1p(jnp.exp(-jnp.abs(dtr_t)))
        if k == 0:
            dt_t = jnp.where((c > 0) | (lane >= FRONT_PAD), dt_t, 0.0)
        dt = jnp.concatenate([dt_t, jnp.zeros((BLOCK - SSM_HEADS, BLOCK), F32)], axis=0).T
        adt = dt * a
        a_cs = sum(jnp.dot(tril_b, t, preferred_element_type=F32) for t in _split3(adt))
        a_last = a_cs[BLOCK - 1:BLOCK, :]
        wx = jnp.exp(a_last - a_cs) * dt
        ea = jnp.where(head_lane, jnp.exp(a_cs), 0.0)
        h1, h2, h3 = _split3(jnp.concatenate([wx, ea], axis=0))
        packed = (h1.astype(F32) + pltpu.roll(h2.astype(F32), SSM_HEADS, axis=1)
                  + pltpu.roll(h3.astype(F32), 2 * SSM_HEADS, axis=1)).astype(BF16)
        both_e = jnp.dot(packed, e_mat, preferred_element_type=F32)
        wx_e = both_e[:BLOCK]
        ea_e = both_e[BLOCK:]
        decay_e = ea_e[BLOCK - 1:BLOCK, :]
        a_cs_t = a_cs.T

        cm_b = cm.astype(BF16)
        bm_b = bm.astype(BF16)
        zero_b = jnp.zeros_like(cm_b)
        cm_split = jnp.concatenate([jnp.where(low, cm_b, zero_b), jnp.where(low, zero_b, cm_b)], axis=0)
        cb_both = lax.dot_general(cm_split, bm_b, nt, preferred_element_type=F32)
        cb_b = [cb_both[:BLOCK].astype(BF16), cb_both[BLOCK:].astype(BF16)]
        dt_tb = dt_t.astype(BF16)

        y_tiles = []
        for j in range(SSM_HEADS // 2):
            xp = xa[j].astype(BF16)
            zx = jnp.zeros_like(xp)
            rhs = jnp.concatenate([jnp.where(low, xp, zx), jnp.where(low, zx, xp)], axis=0)
            ws = []
            for hd in (2 * j, 2 * j + 1):
                seg = a_cs[:, hd:hd + 1] - a_cs_t[hd:hd + 1, :]
                decay = jnp.exp(jnp.where(tril, seg, NEG))
                ws.append(cb_b[hd // hpg] * decay.astype(BF16) * dt_tb[hd:hd + 1, :])
            y_tiles.append(jnp.dot(jnp.concatenate(ws, axis=1), rhs, preferred_element_type=F32))
        y = jnp.concatenate(y_tiles, axis=1)

        y = y + jnp.dot(cm_b, st.astype(BF16), preferred_element_type=F32) * ea_e
        bm_t = bm.T.astype(BF16)
        zero_t = jnp.zeros_like(bm_t)
        xw = jnp.concatenate([t.astype(BF16) for t in xa[:SSM_INNER // LANES]], axis=1) * wx_e.astype(BF16)
        upd = jnp.concatenate(
            [jnp.dot(jnp.where(first_group, bm_t, zero_t), xw[:, :gw], preferred_element_type=F32),
             jnp.dot(jnp.where(first_group, zero_t, bm_t), xw[:, gw:], preferred_element_type=F32)], axis=1)
        st = st * decay_e + upd

        y = y + xs * dsk_ref[...]
        y = y * _silu(z_ref[0, rows, :])
        outs = []
        for g in range(SSM_GROUPS):
            yg = y[:, g * gw:(g + 1) * gw]
            outs.append(yg * lax.rsqrt(jnp.mean(yg * yg, axis=-1, keepdims=True) + EPS))
        o_ref[0, rows, :] = (jnp.concatenate(outs, axis=1) * nw_ref[...]).astype(BF16)

    state[...] = st
    for j in range(n_tiles):
        xpad[j, 0:8, :] = xpad[j, n_chunks * BLOCK:n_chunks * BLOCK + 8, :]


def _ssd(xbc, dt, z, cw, cb, dtb, alog, dsk, nw, e_mat, tm):
    bsz, lp, _ = xbc.shape
    blk = lambda width: pl.BlockSpec((1, tm, width), lambda b, c: (b, c, 0))
    return pl.pallas_call(
        _ssd_kernel,
        grid=(bsz, lp // tm),
        in_specs=[blk(XBC_W), blk(LANES), blk(SSM_INNER),
                  _const_spec((SSM_CONV, XBC_W)), _const_spec((1, XBC_W)), _const_spec((1, LANES)),
                  _const_spec((1, LANES)), _const_spec((1, SSM_INNER)), _const_spec((1, SSM_INNER)),
                  _const_spec((LANES, SSM_INNER))],
        out_specs=blk(SSM_INNER),
        out_shape=jax.ShapeDtypeStruct((bsz, lp, SSM_INNER), BF16),
        scratch_shapes=[pltpu.VMEM((BLOCK, SSM_INNER), F32), pltpu.VMEM((XBC_W // LANES, 8 + tm, LANES), F32)],
        compiler_params=_params("parallel", "arbitrary"),
        name="ssd",
    )(xbc, dt, z, cw, cb, dtb, alog, dsk, nw, e_mat)


def _mlp_tail(o_ref, g_ref, wu_ref, wd_ref):
    x = o_ref[...]
    u = _rms(x, g_ref[...]).astype(BF16)
    acc = None
    for c in range(D_FF // FF_CHUNK):
        a = jnp.dot(u, wu_ref[:, c * FF_CHUNK:(c + 1) * FF_CHUNK], preferred_element_type=F32)
        a = jnp.maximum(a, 0.0)
        a = (a * a).astype(BF16)
        d = jnp.dot(a, wd_ref[c * FF_CHUNK:(c + 1) * FF_CHUNK, :], preferred_element_type=F32)
        acc = d if acc is None else acc + d
    o_ref[...] = x + acc


def _post_even_kernel(load, n_src, *refs):
    att_ref, ssm_ref, wa_ref, ws_ref, g_ref, wu_ref, wd_ref, o_ref = refs[n_src:]
    o_ref[...] = (load(refs[:n_src]) + jnp.dot(att_ref[...], wa_ref[...], preferred_element_type=F32)
                  + jnp.dot(ssm_ref[...], ws_ref[...], preferred_element_type=F32))
    _mlp_tail(o_ref, g_ref, wu_ref, wd_ref)


def _post_even(src, att, ssm, wa, ws, li, g, wu, wd, layer, tm, tiles_per_batch):
    rows = att.shape[0]
    row = lambda width: pl.BlockSpec((tm, width), lambda i: (i, 0))
    src_specs, src_args, load = _stream(src, tm, tiles_per_batch)
    return pl.pallas_call(
        functools.partial(_post_even_kernel, load, len(src_args)),
        grid=(rows // tm,),
        in_specs=src_specs + [row(Q_W), row(SSM_INNER), _const_spec((Q_W, D_MODEL), li),
                              _const_spec((SSM_INNER, D_MODEL), li), _const_spec((1, D_MODEL)),
                              _const_spec((D_MODEL, D_FF), layer), _const_spec((D_FF, D_MODEL), layer)],
        out_specs=row(D_MODEL),
        out_shape=jax.ShapeDtypeStruct((rows, D_MODEL), F32),
        compiler_params=_params("parallel"),
        name="post_even",
    )(*src_args, att, ssm, wa, ws, g, wu, wd)


def _glu_kernel(h_ref, g_ref, w_ref, b_ref, o_ref):
    u = _rms(h_ref[...], g_ref[...]).astype(BF16)
    p = jnp.dot(u, w_ref[...], preferred_element_type=F32) + b_ref[...]
    o_ref[...] = p[:, :D_MODEL] * _sigmoid(p[:, D_MODEL:])


def _glu(h, g, w, li, b, tm):
    rows = h.shape[0]
    row = pl.BlockSpec((tm, D_MODEL), lambda i: (i, 0))
    return pl.pallas_call(
        _glu_kernel,
        grid=(rows // tm,),
        in_specs=[row, _const_spec((1, D_MODEL)), _const_spec((D_MODEL, 2 * D_MODEL), li),
                  _const_spec((1, 2 * D_MODEL))],
        out_specs=row,
        out_shape=jax.ShapeDtypeStruct((rows, D_MODEL), F32),
        compiler_params=_params("parallel"),
        name="glu",
    )(h, g, w, b)


def _conv_fill(main, halo, buf):
    tm = main.shape[0]
    for j in range(D_MODEL // LANES):
        buf[j, 0:HALO, :] = halo[:, j * LANES:(j + 1) * LANES]
        buf[j, HALO:HALO + tm, :] = main[:, j * LANES:(j + 1) * LANES]


def _conv_unit(dw_ref, db_ref, buf, cv, r0, j):
    cols = slice(j * LANES, (j + 1) * LANES)
    acc = jnp.broadcast_to(db_ref[:, cols], (CONV_ROWS, LANES))
    for k in range(CONF_KERNEL):
        off = HALO - (CONF_KERNEL - 1) + k
        acc = acc + dw_ref[k:k + 1, cols] * buf[j, pl.ds(r0 + off, CONV_ROWS), :]
    cv[pl.ds(r0, CONV_ROWS), cols] = acc


def _post_odd_kernel(tiles_per_batch, final, h_ref, u_ref, halo_ref, dw_ref, db_ref, lg_ref, lb_ref, w2_ref,
                     b2_ref, g_ref, wu_ref, wd_ref, o_ref, buf, cv):
    tm = h_ref.shape[0]
    first = (pl.program_id(0) % tiles_per_batch) == 0
    if final:
        hrow = lax.broadcasted_iota(jnp.int32, (HALO, 1), 0)
        halo = jnp.where(first & (hrow < HALO - N_META), 0.0, halo_ref[...])
        main = u_ref[...]
    else:
        row = lax.broadcasted_iota(jnp.int32, (tm, 1), 0)
        halo = jnp.where(first, 0.0, halo_ref[...])
        main = jnp.where(first & (row < FRONT_PAD), 0.0, u_ref[...])
    _conv_fill(main, halo, buf)

    def body(r, carry):
        for j in range(D_MODEL // LANES):
            _conv_unit(dw_ref, db_ref, buf, cv, pl.multiple_of(r * CONV_ROWS, CONV_ROWS), j)
        return carry

    lax.fori_loop(0, tm // CONV_ROWS, body, 0)
    c = cv[...]
    mu = jnp.mean(c, axis=-1, keepdims=True)
    d = c - mu
    var = jnp.mean(d * d, axis=-1, keepdims=True)
    y = _silu(d * lax.rsqrt(var + LN_EPS) * lg_ref[...] + lb_ref[...])
    o_ref[...] = h_ref[...] + jnp.dot(y.astype(BF16), w2_ref[...], preferred_element_type=F32) + b2_ref[...]
    _mlp_tail(o_ref, g_ref, wu_ref, wd_ref)


def _post_odd(h, u, dw, db, lg, lb, w2, li, b2, g, wu, wd, layer, tm, tiles_per_batch, final_lp=None):
    out_rows = h.shape[0] if final_lp is None else tiles_per_batch * tm * (h.shape[0] // final_lp)
    row = pl.BlockSpec((tm, D_MODEL), lambda i: (i, 0))
    if final_lp is None:
        per = tm // HALO
        src = row
        halo = pl.BlockSpec((HALO, D_MODEL), lambda i: (jnp.maximum(i * per - 1, 0), 0))
    else:
        def start(i):
            return (i // tiles_per_batch) * final_lp + BLOCK + (i % tiles_per_batch) * tm
        src = pl.BlockSpec((pl.Element(tm), pl.Element(D_MODEL)), lambda i: (pl.multiple_of(start(i), BLOCK), 0))
        halo = pl.BlockSpec((pl.Element(HALO), pl.Element(D_MODEL)),
                            lambda i: (pl.multiple_of(start(i) - HALO, HALO), 0))
    vec = _const_spec((1, D_MODEL))
    return pl.pallas_call(
        functools.partial(_post_odd_kernel, tiles_per_batch, final_lp is not None),
        grid=(out_rows // tm,),
        in_specs=[src, src, halo, _const_spec((CONF_KERNEL, D_MODEL)), vec, vec, vec,
                  _const_spec((D_MODEL, D_MODEL), li), vec, vec,
                  _const_spec((D_MODEL, D_FF), layer), _const_spec((D_FF, D_MODEL), layer)],
        out_specs=row,
        out_shape=jax.ShapeDtypeStruct((out_rows, D_MODEL), F32),
        scratch_shapes=[pltpu.VMEM((D_MODEL // LANES, HALO + tm, LANES), F32), pltpu.VMEM((tm, D_MODEL), F32)],
        compiler_params=_params("parallel"),
        name="post_odd",
    )(h, u, u, dw, db, lg, lb, w2, b2, g, wu, wd)


def _row_tile(blocks_per_batch, most=5):
    return next(k * BLOCK for k in range(most, 0, -1) if blocks_per_batch % k == 0)


def _pair_heads(w, axis):
    shape = w.shape
    split = shape[:axis] + (ATT_KV_HEADS, ATT_GROUP, HEAD_DIM) + shape[axis + 1:]
    return jnp.swapaxes(w.reshape(split), axis, axis + 1).reshape(shape)


def _expand_matrix():
    r = jnp.arange(LANES)[:, None]
    head = jnp.arange(SSM_INNER)[None, :] // SSM_HEAD_DIM
    return ((r < 3 * SSM_HEADS) & (r % SSM_HEADS == head)).astype(BF16)


def _pad_lanes(v, width=LANES):
    return jnp.pad(v, (0, width - v.shape[0]))[None, :]


def kernel(x, meta_tokens, mix_norm_even, w_in, ssm_conv_w, ssm_conv_b, dt_bias, a_log, d_skip, ssm_norm_w, q_norm, k_norm, sinks, w_out, mix_norm_odd, pw1_w, pw1_b, dw_w, dw_b, ln_g, ln_b, pw2_w, pw2_b, mlp_norm, w_up, w_down):
    bsz, seq, _ = x.shape
    assert seq % BLOCK == 0
    lp = seq + BLOCK
    nb = lp // BLOCK
    tm = _row_tile(nb)
    att_tm = _row_tile(nb, most=13)
    tpb = lp // tm
    rows = bsz * lp
    depth = mlp_norm.shape[0]
    assert depth % 2 == 0
    final_tm = next(k * BLOCK for k in (4, 3, 2, 1) if seq % (k * BLOCK) == 0)

    src = ("x", x.reshape(bsz * seq, D_MODEL), meta_tokens.astype(x.dtype))

    e_mat = _expand_matrix()
    s_q, s_xbc = Q_W, Q_W + 2 * KV_W + SSM_INNER + XBC_W
    w_in_b = w_in.astype(BF16)
    w_cat = jnp.concatenate([_pair_heads(w_in_b[:, :, :s_q], 2), w_in_b[:, :, s_q:s_xbc],
                             jnp.pad(w_in_b[:, :, s_xbc:], ((0, 0), (0, 0), (0, LANES - SSM_HEADS)))], axis=2)
    w_out_b = w_out.astype(BF16)
    wa = _pair_heads(w_out_b[:, :Q_W], 1)
    ws = w_out_b[:, Q_W:]
    wu = w_up.astype(BF16)
    wd = w_down.astype(BF16)
    w1 = pw1_w.astype(BF16)
    w2 = pw2_w.astype(BF16)
    glu_tm = 2 * tm if rows % (2 * tm) == 0 else tm

    for layer in range(depth):
        i = layer // 2
        mg = mlp_norm[layer][None, :]
        if layer % 2 == 0:
            qg = jnp.tile(q_norm[i] * (HEAD_DIM ** -0.5), Q_W // HEAD_DIM)[None, :]
            kg = jnp.tile(k_norm[i], KV_W // HEAD_DIM)[None, :]
            q, kv, z, xbc, dt = _inproj(src, rows, mix_norm_even[i][None, :], w_cat, i, qg, kg, tm, tpb)
            att = _attention(sinks[i], q.reshape(bsz, lp, Q_W), kv.reshape(bsz, lp, 2 * LANES), att_tm)
            ssm = _ssd(xbc.reshape(bsz, lp, XBC_W), dt.reshape(bsz, lp, LANES), z.reshape(bsz, lp, SSM_INNER),
                       ssm_conv_w[i], ssm_conv_b[i][None, :], _pad_lanes(dt_bias[i]), _pad_lanes(a_log[i]),
                       jnp.repeat(d_skip[i], SSM_HEAD_DIM)[None, :], ssm_norm_w[i][None, :], e_mat, tm)
            h = _post_even(src, att.reshape(rows, Q_W), ssm.reshape(rows, SSM_INNER), wa, ws, i, mg, wu, wd, layer,
                           tm, tpb)
        else:
            u = _glu(h, mix_norm_odd[i][None, :], w1, i, pw1_b[i][None, :], glu_tm)
            conv = (dw_w[i], dw_b[i][None, :], ln_g[i][None, :], ln_b[i][None, :], w2, i, pw2_b[i][None, :],
                    mg, wu, wd, layer)
            if layer == depth - 1:
                h = _post_odd(h, u, *conv, final_tm, seq // final_tm, final_lp=lp)
            else:
                h = _post_odd(h, u, *conv, tm, tpb)
        src = ("h", h)
    return h.reshape(bsz, seq, D_MODEL)
```

```python
import functools

import jax
import jax.numpy as jnp
from jax import lax
from jax.experimental import pallas as pl
from jax.experimental.pallas import tpu as pltpu

D_MODEL = 1024
N_META = 16
BLOCK = 128
WINDOW = 128
ATT_HEADS = 8
ATT_KV_HEADS = 2
ATT_GROUP = ATT_HEADS // ATT_KV_HEADS
HEAD_DIM = 64
SSM_HEADS = 16
SSM_HEAD_DIM = 64
SSM_INNER = SSM_HEADS * SSM_HEAD_DIM
SSM_GROUPS = 2
SSM_STATE = 64
SSM_CONV = 4
CONF_KERNEL = 31
D_FF = 4 * D_MODEL
EPS = 1e-6
LN_EPS = 1e-5

Q_W = ATT_HEADS * HEAD_DIM
KV_W = ATT_KV_HEADS * HEAD_DIM
BC_W = SSM_GROUPS * SSM_STATE
XBC_W = SSM_INNER + 2 * BC_W
FRONT_PAD = BLOCK - N_META

LANES = 128
HALO = 32
CONV_ROWS = 128
FF_CHUNK = 1024
GLU_CHUNK = 512
VMEM_LIMIT = 56 * 1024 * 1024
NEG = -1e30

BF16 = jnp.bfloat16
F32 = jnp.float32


def _const_spec(shape, layer=None):
    zeros = (0,) * len(shape)
    if layer is None:
        return pl.BlockSpec(shape, lambda *_: zeros, pipeline_mode=pl.Buffered(1))
    return pl.BlockSpec((None,) + tuple(shape), lambda *_: (layer,) + zeros, pipeline_mode=pl.Buffered(1))


def _params(*semantics):
    return pltpu.CompilerParams(dimension_semantics=semantics, vmem_limit_bytes=VMEM_LIMIT)


def _rms(x, g):
    return x * lax.rsqrt(jnp.mean(x * x, axis=-1, keepdims=True) + EPS) * g


def _sigmoid(x):
    return 0.5 + 0.5 * jnp.tanh(0.5 * x)


def _silu(x):
    h = 0.5 * x
    return h + h * jnp.tanh(h)


def _head_rms(x, gain):
    lane = lax.broadcasted_iota(jnp.int32, (1, LANES), 1)
    low = lane < HEAD_DIM
    outs = []
    for t in range(x.shape[1] // LANES):
        xt = x[:, t * LANES:(t + 1) * LANES]
        sq = xt * xt
        lo = jnp.sum(jnp.where(low, sq, 0.0), axis=-1, keepdims=True)
        hi = jnp.sum(jnp.where(low, 0.0, sq), axis=-1, keepdims=True)
        r = jnp.where(low, lax.rsqrt(lo * (1.0 / HEAD_DIM) + EPS), lax.rsqrt(hi * (1.0 / HEAD_DIM) + EPS))
        outs.append(xt * r * gain[:, t * LANES:(t + 1) * LANES])
    return outs[0] if len(outs) == 1 else jnp.concatenate(outs, axis=1)


def _split3(x):
    h1 = x.astype(BF16)
    r1 = x - h1.astype(F32)
    h2 = r1.astype(BF16)
    h3 = (r1 - h2.astype(F32)).astype(BF16)
    return h1, h2, h3


IN_COLS = Q_W + 2 * LANES + SSM_INNER + XBC_W + LANES


def _stream(src, tm, tiles_per_batch):
    if src[0] == "h":
        return [pl.BlockSpec((tm, D_MODEL), lambda i: (i, 0))], [src[1]], lambda refs: refs[0][...]
    _, x, meta = src
    assert tiles_per_batch >= 2
    seq = tiles_per_batch * tm - BLOCK

    def x_index(i):
        start = (i // tiles_per_batch) * seq + jnp.maximum((i % tiles_per_batch) * tm - BLOCK, 0)
        return pl.multiple_of(start, BLOCK), 0

    def load(refs):
        x_ref, meta_ref = refs
        xb = x_ref[...]
        front = jnp.concatenate([jnp.zeros((FRONT_PAD, D_MODEL), F32), meta_ref[...]], axis=0)
        first = jnp.concatenate([front, xb[:tm - BLOCK]], axis=0)
        return jnp.where(pl.program_id(0) % tiles_per_batch == 0, first, xb)

    specs = [pl.BlockSpec((pl.Element(tm), pl.Element(D_MODEL)), x_index), _const_spec((N_META, D_MODEL))]
    return specs, [x, meta], load


def _inproj_kernel(load, n_src, *refs):
    g_ref, w_ref, qg_ref, kg_ref, q_ref, kv_ref, z_ref, xbc_ref, dt_ref = refs[n_src:]
    u = _rms(load(refs[:n_src]), g_ref[...]).astype(BF16)
    p = jnp.dot(u, w_ref[...], preferred_element_type=F32)
    o = 0
    q = _head_rms(p[:, o:o + Q_W], qg_ref[...])
    o += Q_W
    k = _head_rms(p[:, o:o + LANES], kg_ref[...])
    o += LANES
    v = p[:, o:o + LANES]
    o += LANES
    q_ref[...] = q.astype(BF16)
    kv_ref[...] = jnp.concatenate([k, v], axis=1).astype(BF16)
    z_ref[...] = p[:, o:o + SSM_INNER]
    o += SSM_INNER
    xbc_ref[...] = p[:, o:o + XBC_W]
    o += XBC_W
    dt_ref[...] = p[:, o:o + LANES]


def _inproj(src, rows, g, w, li, qg, kg, tm, tiles_per_batch):
    row = lambda width: pl.BlockSpec((tm, width), lambda i: (i, 0))
    src_specs, src_args, load = _stream(src, tm, tiles_per_batch)
    return pl.pallas_call(
        functools.partial(_inproj_kernel, load, len(src_args)),
        grid=(rows // tm,),
        in_specs=src_specs + [_const_spec((1, D_MODEL)), _const_spec((D_MODEL, IN_COLS), li),
                              _const_spec((1, Q_W)), _const_spec((1, LANES))],
        out_specs=[row(Q_W), row(2 * LANES), row(SSM_INNER), row(XBC_W), row(LANES)],
        out_shape=[jax.ShapeDtypeStruct((rows, Q_W), BF16), jax.ShapeDtypeStruct((rows, 2 * LANES), BF16),
                   jax.ShapeDtypeStruct((rows, SSM_INNER), F32), jax.ShapeDtypeStruct((rows, XBC_W), F32),
                   jax.ShapeDtypeStruct((rows, LANES), F32)],
        compiler_params=_params("parallel"),
        name="inproj",
    )(*src_args, g, w, qg, kg)


N_BIAS = 3


def _attn_kernel(sink_ref, q_ref, kv_ref, kvp_ref, kvm_ref, o_ref, bias_ref):
    t = pl.program_id(1)
    n_blocks = q_ref.shape[1] // BLOCK
    nq = ATT_GROUP * BLOCK
    nk = 2 * BLOCK + N_META
    col = lax.broadcasted_iota(jnp.int32, (1, nq), 1)
    grp = col // BLOCK

    @pl.when(t == 0)
    def _():
        for n in range(N_BIAS):
            q_pos = n * BLOCK + (col % BLOCK) - FRONT_PAD
            k_pos = (n - 1) * BLOCK + lax.broadcasted_iota(jnp.int32, (2 * BLOCK, 1), 0) - FRONT_PAD
            dist = q_pos - k_pos
            valid = (k_pos >= N_META) & (dist >= 0) & (dist < WINDOW)
            pen = jnp.abs(dist).astype(F32)
            dist_m = q_pos - lax.broadcasted_iota(jnp.int32, (N_META, 1), 0)
            valid_m = dist_m >= 0
            pen_m = jnp.minimum(jnp.abs(dist_m), WINDOW).astype(F32)
            for kv in range(ATT_KV_HEADS):
                slope = jnp.zeros((1, nq), F32)
                for g in range(ATT_GROUP):
                    slope = jnp.where(grp == g, 2.0 ** (-8.0 * (kv * ATT_GROUP + g + 1) / ATT_HEADS), slope)
                bias_ref[n, kv, 0:2 * BLOCK, :] = jnp.where(valid, -slope * pen, NEG)
                bias_ref[n, kv, 2 * BLOCK:nk, :] = jnp.where(valid_m, -slope * pen_m, NEG)

    lane = lax.broadcasted_iota(jnp.int32, (1, LANES), 1)
    sub = lax.broadcasted_iota(jnp.int32, (LANES, 1), 0)
    nt = (((1,), (1,)), ((), ()))
    halves = [lane < HEAD_DIM, lane >= HEAD_DIM]
    sinks = []
    for kv in range(ATT_KV_HEADS):
        sink = jnp.zeros((1, nq), F32)
        for g in range(ATT_GROUP):
            sink = jnp.where(grp == g, sink_ref[kv * ATT_GROUP + g], sink)
        sinks.append(sink)

    def transposed(v):
        return v.astype(F32).T.astype(BF16)

    kvm = kvm_ref[0, FRONT_PAD:BLOCK, :]
    k_meta = kvm[:, :LANES]
    vt_meta = transposed(kvm[:, LANES:])
    kv_prev = kvp_ref[0]
    k_prev = kv_prev[:, :LANES]
    vt_prev = transposed(kv_prev[:, LANES:])
    for b in range(n_blocks):
        rows = slice(b * BLOCK, (b + 1) * BLOCK)
        table = (N_BIAS - 1) if b >= N_BIAS - 1 else jnp.where(t == 0, b, N_BIAS - 1)
        q = q_ref[0, rows, :]
        qs = jnp.concatenate([q[:, g * LANES:(g + 1) * LANES] for g in range(ATT_GROUP)], axis=0)
        kv_cur = kv_ref[0, rows, :]
        k_cur = kv_cur[:, :LANES]
        vt_cur = transposed(kv_cur[:, LANES:])
        k_all = jnp.concatenate([k_prev, k_cur, k_meta], axis=0)
        v_t = jnp.concatenate([vt_prev, vt_cur, vt_meta], axis=1)
        zk = jnp.zeros_like(k_all)
        km = jnp.concatenate([jnp.where(halves[0], k_all, zk), jnp.where(halves[1], k_all, zk)], axis=0)
        s_both = lax.dot_general(km, qs, nt, preferred_element_type=F32)
        out_t = None
        for kv in range(ATT_KV_HEADS):
            s = s_both[kv * nk:(kv + 1) * nk] + bias_ref[table, kv]
            mx = jnp.maximum(jnp.max(s, axis=0, keepdims=True), sinks[kv])
            e = jnp.exp(s - mx)
            den = jnp.sum(e, axis=0, keepdims=True) + jnp.exp(sinks[kv] - mx)
            o_t = jnp.dot(v_t, e.astype(BF16), preferred_element_type=F32) * (1.0 / den)
            out_t = o_t if out_t is None else jnp.where(sub < HEAD_DIM, out_t, o_t)
        o_ref[0, rows, :] = jnp.concatenate([out_t[:, g * BLOCK:(g + 1) * BLOCK].T for g in range(ATT_GROUP)],
                                            axis=1).astype(BF16)
        k_prev, vt_prev = k_cur, vt_cur


def _attention(sinks, q, kv, tm):
    bsz, lp, _ = q.shape
    per = tm // BLOCK
    return pl.pallas_call(
        _attn_kernel,
        grid=(bsz, lp // tm),
        in_specs=[pl.BlockSpec(memory_space=pltpu.SMEM),
                  pl.BlockSpec((1, tm, Q_W), lambda b, t: (b, t, 0)),
                  pl.BlockSpec((1, tm, 2 * LANES), lambda b, t: (b, t, 0)),
                  pl.BlockSpec((1, BLOCK, 2 * LANES), lambda b, t: (b, jnp.maximum(t * per - 1, 0), 0)),
                  pl.BlockSpec((1, BLOCK, 2 * LANES), lambda b, t: (b, 0, 0))],
        out_specs=pl.BlockSpec((1, tm, Q_W), lambda b, t: (b, t, 0)),
        out_shape=jax.ShapeDtypeStruct((bsz, lp, Q_W), BF16),
        scratch_shapes=[pltpu.VMEM((N_BIAS, ATT_KV_HEADS, 2 * BLOCK + N_META, ATT_GROUP * BLOCK), F32)],
        compiler_params=_params("parallel", "arbitrary"),
        name="attention",
    )(sinks, q, kv, kv, kv)


def _ssd_kernel(xbc_ref, dt_ref, z_ref, cw_ref, cb_ref, dtb_ref, alog_ref, dsk_ref, nw_ref, e_ref,
                o_ref, state, xpad):
    c = pl.program_id(1)
    n_chunks = xbc_ref.shape[1] // BLOCK
    n_tiles = XBC_W // LANES

    for j in range(n_tiles):
        xpad[j, 8:, :] = xbc_ref[0, :, j * LANES:(j + 1) * LANES]

    @pl.when(c == 0)
    def _():
        state[...] = jnp.zeros_like(state)
        for j in range(n_tiles):
            xpad[j, 0:8 + FRONT_PAD, :] = jnp.zeros((8 + FRONT_PAD, LANES), F32)

    ri = lax.broadcasted_iota(jnp.int32, (BLOCK, BLOCK), 0)
    ci = lax.broadcasted_iota(jnp.int32, (BLOCK, BLOCK), 1)
    tril = ri >= ci
    tril_b = jnp.where(tril, 1.0, 0.0).astype(BF16)
    lane = lax.broadcasted_iota(jnp.int32, (1, LANES), 1)
    low = lane < SSM_STATE
    head_lane = lane < SSM_HEADS
    first_group = lax.broadcasted_iota(jnp.int32, (BLOCK, 1), 0) < SSM_STATE
    nt = (((1,), (1,)), ((), ()))
    hpg = SSM_HEADS // SSM_GROUPS
    gw = SSM_INNER // SSM_GROUPS
    a = -jnp.exp(alog_ref[...])
    e_mat = e_ref[...]
    st = state[...]

    for k in range(n_chunks):
        r0 = 8 + k * BLOCK
        rows = slice(k * BLOCK, (k + 1) * BLOCK)
        xa = []
        for j in range(n_tiles):
            cols = slice(j * LANES, (j + 1) * LANES)
            acc = cb_ref[:, cols]
            for s in range(SSM_CONV):
                acc = acc + cw_ref[SSM_CONV - 1 - s:SSM_CONV - s, cols] * xpad[j, r0 - s:r0 - s + BLOCK, :]
            xa.append(_silu(acc))
        xs = jnp.concatenate(xa[:SSM_INNER // LANES], axis=1)
        bm = xa[SSM_INNER // LANES]
        cm = xa[SSM_INNER // LANES + 1]

        dtr_t = (dt_ref[0, rows, :] + dtb_ref[...]).T[0:SSM_HEADS, :]
        dt_t = jnp.maximum(dtr_t, 0.0) + jnp.log1p(jnp.exp(-jnp.abs(dtr_t)))
        if k == 0:
            dt_t = jnp.where((c > 0) | (lane >= FRONT_PAD), dt_t, 0.0)
        dt = jnp.concatenate([dt_t, jnp.zeros((BLOCK - SSM_HEADS, BLOCK), F32)], axis=0).T
        adt = dt * a
        a_cs = sum(jnp.dot(tril_b, t, preferred_element_type=F32) for t in _split3(adt))
        a_last = a_cs[BLOCK - 1:BLOCK, :]
        wx = jnp.exp(a_last - a_cs) * dt
        ea = jnp.where(head_lane, jnp.exp(a_cs), 0.0)
        h1, h2, h3 = _split3(jnp.concatenate([wx, ea], axis=0))
        packed = (h1.astype(F32) + pltpu.roll(h2.astype(F32), SSM_HEADS, axis=1)
                  + pltpu.roll(h3.astype(F32), 2 * SSM_HEADS, axis=1)).astype(BF16)
        both_e = jnp.dot(packed, e_mat, preferred_element_type=F32)
        wx_e = both_e[:BLOCK]
        ea_e = both_e[BLOCK:]
        decay_e = ea_e[BLOCK - 1:BLOCK, :]
        a_cs_t = a_cs.T

        cm_b = cm.astype(BF16)
        bm_b = bm.astype(BF16)
        zero_b = jnp.zeros_like(cm_b)
        cm_split = jnp.concatenate([jnp.where(low, cm_b, zero_b), jnp.where(low, zero_b, cm_b)], axis=0)
        cb_both = lax.dot_general(cm_split, bm_b, nt, preferred_element_type=F32)
        cb_b = [cb_both[:BLOCK].astype(BF16), cb_both[BLOCK:].astype(BF16)]
        dt_tb = dt_t.astype(BF16)

        y_tiles = []
        for j in range(SSM_HEADS // 2):
            xp = xa[j].astype(BF16)
            zx = jnp.zeros_like(xp)
            rhs = jnp.concatenate([jnp.where(low, xp, zx), jnp.where(low, zx, xp)], axis=0)
            ws = []
            for hd in (2 * j, 2 * j + 1):
                seg = a_cs[:, hd:hd + 1] - a_cs_t[hd:hd + 1, :]
                decay = jnp.exp(jnp.where(tril, seg, NEG))
                ws.append(cb_b[hd // hpg] * decay.astype(BF16) * dt_tb[hd:hd + 1, :])
            y_tiles.append(jnp.dot(jnp.concatenate(ws, axis=1), rhs, preferred_element_type=F32))
        y = jnp.concatenate(y_tiles, axis=1)

        y = y + jnp.dot(cm_b, st.astype(BF16), preferred_element_type=F32) * ea_e
        bm_t = bm.T.astype(BF16)
        zero_t = jnp.zeros_like(bm_t)
        xw = jnp.concatenate([t.astype(BF16) for t in xa[:SSM_INNER // LANES]], axis=1) * wx_e.astype(BF16)
        upd = jnp.concatenate(
            [jnp.dot(jnp.where(first_group, bm_t, zero_t), xw[:, :gw], preferred_element_type=F32),
             jnp.dot(jnp.where(first_group, zero_t, bm_t), xw[:, gw:], preferred_element_type=F32)], axis=1)
        st = st * decay_e + upd

        y = y + xs * dsk_ref[...]
        y = y * _silu(z_ref[0, rows, :])
        outs = []
        for g in range(SSM_GROUPS):
            yg = y[:, g * gw:(g + 1) * gw]
            outs.append(yg * lax.rsqrt(jnp.mean(yg * yg, axis=-1, keepdims=True) + EPS))
        o_ref[0, rows, :] = (jnp.concatenate(outs, axis=1) * nw_ref[...]).astype(BF16)

    state[...] = st
    for j in range(n_tiles):
        xpad[j, 0:8, :] = xpad[j, n_chunks * BLOCK:n_chunks * BLOCK + 8, :]


def _ssd(xbc, dt, z, cw, cb, dtb, alog, dsk, nw, e_mat, tm):
    bsz, lp, _ = xbc.shape
    blk = lambda width: pl.BlockSpec((1, tm, width), lambda b, c: (b, c, 0))
    return pl.pallas_call(
        _ssd_kernel,
        grid=(bsz, lp // tm),
        in_specs=[blk(XBC_W), blk(LANES), blk(SSM_INNER),
                  _const_spec((SSM_CONV, XBC_W)), _const_spec((1, XBC_W)), _const_spec((1, LANES)),
                  _const_spec((1, LANES)), _const_spec((1, SSM_INNER)), _const_spec((1, SSM_INNER)),
                  _const_spec((LANES, SSM_INNER))],
        out_specs=blk(SSM_INNER),
        out_shape=jax.ShapeDtypeStruct((bsz, lp, SSM_INNER), BF16),
        scratch_shapes=[pltpu.VMEM((BLOCK, SSM_INNER), F32), pltpu.VMEM((XBC_W // LANES, 8 + tm, LANES), F32)],
        compiler_params=_params("parallel", "arbitrary"),
        name="ssd",
    )(xbc, dt, z, cw, cb, dtb, alog, dsk, nw, e_mat)


def _mlp_tail(o_ref, g_ref, wu_ref, wd_ref):
    x = o_ref[...]
    u = _rms(x, g_ref[...]).astype(BF16)
    acc = None
    for c in range(D_FF // FF_CHUNK):
        a = jnp.dot(u, wu_ref[:, c * FF_CHUNK:(c + 1) * FF_CHUNK], preferred_element_type=F32)
        a = jnp.maximum(a, 0.0)
        a = (a * a).astype(BF16)
        d = jnp.dot(a, wd_ref[c * FF_CHUNK:(c + 1) * FF_CHUNK, :], preferred_element_type=F32)
        acc = d if acc is None else acc + d
    o_ref[...] = x + acc


def _post_even_kernel(load, n_src, *refs):
    (att_ref, ssm_ref, wa_ref, ws_ref, g_ref, wu_ref, wd_ref, g2_ref, w1_ref, b1_ref, o_ref, u_ref) = refs[n_src:]
    o_ref[...] = (load(refs[:n_src]) + jnp.dot(att_ref[...], wa_ref[...], preferred_element_type=F32)
                  + jnp.dot(ssm_ref[...], ws_ref[...], preferred_element_type=F32))
    _mlp_tail(o_ref, g_ref, wu_ref, wd_ref)
    un = _rms(o_ref[...], g2_ref[...]).astype(BF16)
    for c in range(D_MODEL // GLU_CHUNK):
        val = slice(c * GLU_CHUNK, (c + 1) * GLU_CHUNK)
        gate = slice(D_MODEL + c * GLU_CHUNK, D_MODEL + (c + 1) * GLU_CHUNK)
        pv = jnp.dot(un, w1_ref[:, val], preferred_element_type=F32) + b1_ref[:, val]
        pg = jnp.dot(un, w1_ref[:, gate], preferred_element_type=F32) + b1_ref[:, gate]
        u_ref[:, val] = pv * _sigmoid(pg)


def _post_even(src, att, ssm, wa, ws, li, g, wu, wd, layer, g2, w1, b1, tm, tiles_per_batch):
    rows = att.shape[0]
    row = lambda width: pl.BlockSpec((tm, width), lambda i: (i, 0))
    src_specs, src_args, load = _stream(src, tm, tiles_per_batch)
    return pl.pallas_call(
        functools.partial(_post_even_kernel, load, len(src_args)),
        grid=(rows // tm,),
        in_specs=src_specs + [row(Q_W), row(SSM_INNER), _const_spec((Q_W, D_MODEL), li),
                              _const_spec((SSM_INNER, D_MODEL), li), _const_spec((1, D_MODEL)),
                              _const_spec((D_MODEL, D_FF), layer), _const_spec((D_FF, D_MODEL), layer),
                              _const_spec((1, D_MODEL)), _const_spec((D_MODEL, 2 * D_MODEL), li),
                              _const_spec((1, 2 * D_MODEL))],
        out_specs=[row(D_MODEL), row(D_MODEL)],
        out_shape=[jax.ShapeDtypeStruct((rows, D_MODEL), F32), jax.ShapeDtypeStruct((rows, D_MODEL), F32)],
        compiler_params=_params("parallel"),
        name="post_even",
    )(*src_args, att, ssm, wa, ws, g, wu, wd, g2, w1, b1)


def _conv_fill(main, halo, buf):
    tm = main.shape[0]
    for j in range(D_MODEL // LANES):
        buf[j, 0:HALO, :] = halo[:, j * LANES:(j + 1) * LANES]
        buf[j, HALO:HALO + tm, :] = main[:, j * LANES:(j + 1) * LANES]


def _conv_unit(dw_ref, db_ref, buf, cv, r0, j):
    cols = slice(j * LANES, (j + 1) * LANES)
    acc = jnp.broadcast_to(db_ref[:, cols], (CONV_ROWS, LANES))
    for k in range(CONF_KERNEL):
        off = HALO - (CONF_KERNEL - 1) + k
        acc = acc + dw_ref[k:k + 1, cols] * buf[j, pl.ds(r0 + off, CONV_ROWS), :]
    cv[pl.ds(r0, CONV_ROWS), cols] = acc


def _post_odd_kernel(tiles_per_batch, final, h_ref, u_ref, halo_ref, dw_ref, db_ref, lg_ref, lb_ref, w2_ref,
                     b2_ref, g_ref, wu_ref, wd_ref, o_ref, buf, cv):
    tm = h_ref.shape[0]
    first = (pl.program_id(0) % tiles_per_batch) == 0
    if final:
        hrow = lax.broadcasted_iota(jnp.int32, (HALO, 1), 0)
        halo = jnp.where(first & (hrow < HALO - N_META), 0.0, halo_ref[...])
        main = u_ref[...]
    else:
        row = lax.broadcasted_iota(jnp.int32, (tm, 1), 0)
        halo = jnp.where(first, 0.0, halo_ref[...])
        main = jnp.where(first & (row < FRONT_PAD), 0.0, u_ref[...])
    _conv_fill(main, halo, buf)

    def body(r, carry):
        for j in range(D_MODEL // LANES):
            _conv_unit(dw_ref, db_ref, buf, cv, pl.multiple_of(r * CONV_ROWS, CONV_ROWS), j)
        return carry

    lax.fori_loop(0, tm // CONV_ROWS, body, 0)
    c = cv[...]
    mu = jnp.mean(c, axis=-1, keepdims=True)
    d = c - mu
    var = jnp.mean(d * d, axis=-1, keepdims=True)
    y = _silu(d * lax.rsqrt(var + LN_EPS) * lg_ref[...] + lb_ref[...])
    o_ref[...] = h_ref[...] + jnp.dot(y.astype(BF16), w2_ref[...], preferred_element_type=F32) + b2_ref[...]
    _mlp_tail(o_ref, g_ref, wu_ref, wd_ref)


def _post_odd(h, u, dw, db, lg, lb, w2, li, b2, g, wu, wd, layer, tm, tiles_per_batch, final_lp=None):
    out_rows = h.shape[0] if final_lp is None else tiles_per_batch * tm * (h.shape[0] // final_lp)
    row = pl.BlockSpec((tm, D_MODEL), lambda i: (i, 0))
    if final_lp is None:
        per = tm // HALO
        src = row
        halo = pl.BlockSpec((HALO, D_MODEL), lambda i: (jnp.maximum(i * per - 1, 0), 0))
    else:
        def start(i):
            return (i // tiles_per_batch) * final_lp + BLOCK + (i % tiles_per_batch) * tm
        src = pl.BlockSpec((pl.Element(tm), pl.Element(D_MODEL)), lambda i: (pl.multiple_of(start(i), BLOCK), 0))
        halo = pl.BlockSpec((pl.Element(HALO), pl.Element(D_MODEL)),
                            lambda i: (pl.multiple_of(start(i) - HALO, HALO), 0))
    vec = _const_spec((1, D_MODEL))
    return pl.pallas_call(
        functools.partial(_post_odd_kernel, tiles_per_batch, final_lp is not None),
        grid=(out_rows // tm,),
        in_specs=[src, src, halo, _const_spec((CONF_KERNEL, D_MODEL)), vec, vec, vec,
                  _const_spec((D_MODEL, D_MODEL), li), vec, vec,
                  _const_spec((D_MODEL, D_FF), layer), _const_spec((D_FF, D_MODEL), layer)],
        out_specs=row,
        out_shape=jax.ShapeDtypeStruct((out_rows, D_MODEL), F32),
        scratch_shapes=[pltpu.VMEM((D_MODEL // LANES, HALO + tm, LANES), F32), pltpu.VMEM((tm, D_MODEL), F32)],
        compiler_params=_params("parallel"),
        name="post_odd",
    )(h, u, u, dw, db, lg, lb, w2, b2, g, wu, wd)


def _row_tile(blocks_per_batch, most=5):
    return next(k * BLOCK for k in range(most, 0, -1) if blocks_per_batch % k == 0)


def _pair_heads(w, axis):
    shape = w.shape
    split = shape[:axis] + (ATT_KV_HEADS, ATT_GROUP, HEAD_DIM) + shape[axis + 1:]
    return jnp.swapaxes(w.reshape(split), axis, axis + 1).reshape(shape)


def _expand_matrix():
    r = jnp.arange(LANES)[:, None]
    head = jnp.arange(SSM_INNER)[None, :] // SSM_HEAD_DIM
    return ((r < 3 * SSM_HEADS) & (r % SSM_HEADS == head)).astype(BF16)


def _pad_lanes(v, width=LANES):
    return jnp.pad(v, (0, width - v.shape[0]))[None, :]


def kernel(x, meta_tokens, mix_norm_even, w_in, ssm_conv_w, ssm_conv_b, dt_bias, a_log, d_skip, ssm_norm_w, q_norm, k_norm, sinks, w_out, mix_norm_odd, pw1_w, pw1_b, dw_w, dw_b, ln_g, ln_b, pw2_w, pw2_b, mlp_norm, w_up, w_down):
    bsz, seq, _ = x.shape
    assert seq % BLOCK == 0
    lp = seq + BLOCK
    nb = lp // BLOCK
    tm = _row_tile(nb)
    att_tm = _row_tile(nb, most=13)
    tpb = lp // tm
    rows = bsz * lp
    depth = mlp_norm.shape[0]
    assert depth % 2 == 0
    final_tm = next(k * BLOCK for k in (4, 3, 2, 1) if seq % (k * BLOCK) == 0)

    src = ("x", x.reshape(bsz * seq, D_MODEL), meta_tokens.astype(x.dtype))

    e_mat = _expand_matrix()
    s_q, s_xbc = Q_W, Q_W + 2 * KV_W + SSM_INNER + XBC_W
    w_in_b = w_in.astype(BF16)
    w_cat = jnp.concatenate([_pair_heads(w_in_b[:, :, :s_q], 2), w_in_b[:, :, s_q:s_xbc],
                             jnp.pad(w_in_b[:, :, s_xbc:], ((0, 0), (0, 0), (0, LANES - SSM_HEADS)))], axis=2)
    w_out_b = w_out.astype(BF16)
    wa = _pair_heads(w_out_b[:, :Q_W], 1)
    ws = w_out_b[:, Q_W:]
    wu = w_up.astype(BF16)
    wd = w_down.astype(BF16)
    w1 = pw1_w.astype(BF16)
    w2 = pw2_w.astype(BF16)

    for layer in range(depth):
        i = layer // 2
        mg = mlp_norm[layer][None, :]
        if layer % 2 == 0:
            qg = jnp.tile(q_norm[i] * (HEAD_DIM ** -0.5), Q_W // HEAD_DIM)[None, :]
            kg = jnp.tile(k_norm[i], KV_W // HEAD_DIM)[None, :]
            q, kv, z, xbc, dt = _inproj(src, rows, mix_norm_even[i][None, :], w_cat, i, qg, kg, tm, tpb)
            att = _attention(sinks[i], q.reshape(bsz, lp, Q_W), kv.reshape(bsz, lp, 2 * LANES), att_tm)
            ssm = _ssd(xbc.reshape(bsz, lp, XBC_W), dt.reshape(bsz, lp, LANES), z.reshape(bsz, lp, SSM_INNER),
                       ssm_conv_w[i], ssm_conv_b[i][None, :], _pad_lanes(dt_bias[i]), _pad_lanes(a_log[i]),
                       jnp.repeat(d_skip[i], SSM_HEAD_DIM)[None, :], ssm_norm_w[i][None, :], e_mat, tm)
            h, u = _post_even(src, att.reshape(rows, Q_W), ssm.reshape(rows, SSM_INNER), wa, ws, i, mg, wu, wd, layer,
                              mix_norm_odd[i][None, :], w1, pw1_b[i][None, :], tm, tpb)
        else:
            conv = (dw_w[i], dw_b[i][None, :], ln_g[i][None, :], ln_b[i][None, :], w2, i, pw2_b[i][None, :],
                    mg, wu, wd, layer)
            if layer == depth - 1:
                h = _post_odd(h, u, *conv, final_tm, seq // final_tm, final_lp=lp)
            else:
                h = _post_odd(h, u, *conv, tm, tpb)
        src = ("h", h)
    return h.reshape(bsz, seq, D_MODEL)
```
